```python
import jax, jax.numpy as jnp
from jax import lax
import numpy as np

D_MODEL = 1024
BATCH = 16
SEQ = 4096
DEPTH = 1

SSD_EXPAND = 2
SSD_D_INNER = SSD_EXPAND * D_MODEL
SSD_HEAD_DIM = 64
SSD_N_HEADS = SSD_D_INNER // SSD_HEAD_DIM
SSD_N_GROUPS = 8
SSD_HEADS_PER_GROUP = SSD_N_HEADS // SSD_N_GROUPS
SSD_D_STATE = 128
SSD_CONV_WIDTH = 5
SSD_CHUNK = 128
SSD_BC_WIDTH = SSD_N_GROUPS * SSD_D_STATE
SSD_CONV_CH = SSD_D_INNER + 2 * SSD_BC_WIDTH
ATT_HEAD_DIM = 64
ATT_HEADS_PER_GROUP = 8
ATT_PATTERNS = ((128, 1), (512, 4), (2048, 16))
ATT_N_GROUPS = 3
ATT_N_HEADS = ATT_N_GROUPS * ATT_HEADS_PER_GROUP
ATT_WIDTH = ATT_N_HEADS * ATT_HEAD_DIM
ATT_OUT_WIDTH = ATT_HEADS_PER_GROUP * ATT_HEAD_DIM
ATT_BLOCK = 64
ROPE_THETA = 500000.0
ROPE_DIM = ATT_HEAD_DIM // 4
NEG_BIG = -1e30
FFN_HIDDEN = (((8 * D_MODEL + 2) // 3 + 255) // 256) * 256
NORM_EPS = 1e-6
IN_SIZES = (SSD_D_INNER, SSD_CONV_CH, 2 * SSD_N_HEADS, 3 * ATT_WIDTH, 2 * D_MODEL)
IN_COLS = SSD_D_INNER + SSD_CONV_CH + 2 * SSD_N_HEADS + 3 * ATT_WIDTH + 2 * D_MODEL

kernel_name = "hybrid_ssd_dilated_attn_block"


def split_columns(t, sizes):
    outs, start = [], 0
    for s in sizes:
        outs.append(t[..., start:start + s])
        start += s
    return outs


def rms_norm(x, w):
    xf = x.astype(jnp.float32)
    y = xf * lax.rsqrt(jnp.mean(xf * xf, axis=-1, keepdims=True) + NORM_EPS)
    return (y * w.astype(jnp.float32)).astype(x.dtype)


def partial_rotary(x):
    S = x.shape[1]
    pos = jnp.arange(S, dtype=jnp.float32)
    inv_freq = ROPE_THETA ** (-jnp.arange(0, ROPE_DIM, 2, dtype=jnp.float32) / ROPE_DIM)
    ang = pos[:, None] * inv_freq[None, :]
    ang = jnp.concatenate([ang, ang], axis=-1)[None, :, None, :]
    xr, xp = x[..., :ROPE_DIM].astype(jnp.float32), x[..., ROPE_DIM:]
    x1, x2 = xr[..., :ROPE_DIM // 2], xr[..., ROPE_DIM // 2:]
    rot = jnp.concatenate([-x2, x1], axis=-1)
    xr = xr * jnp.cos(ang) + rot * jnp.sin(ang)
    return jnp.concatenate([xr.astype(x.dtype), xp], axis=-1)


def centred_depthwise_conv(x, w, b):
    K, C = w.shape
    y = lax.conv_general_dilated(x, w[:, None, :].astype(x.dtype), window_strides=(1,),
                                 padding=[(K // 2, K // 2)],
                                 dimension_numbers=('NWC', 'WIO', 'NWC'),
                                 feature_group_count=C)
    return y + b.astype(x.dtype)


def ssd_chunked_scan(xh, dt, A, Bm, Cm):
    Bsz, S, G, J, P = xh.shape
    N = Bm.shape[-1]
    nc = S // SSD_CHUNK

    def to_chunks(t):
        return t.reshape((Bsz, nc, SSD_CHUNK) + t.shape[2:]).swapaxes(0, 1)

    dA = dt * A
    xs = (to_chunks(xh), to_chunks(dt), to_chunks(dA), to_chunks(Bm), to_chunks(Cm))
    lower = jnp.tril(jnp.ones((SSD_CHUNK, SSD_CHUNK), dtype=bool))[None, :, :, None, None]
    state0 = jnp.zeros((Bsz, G, J, P, N), jnp.float32)

    def step(state, inp):
        xc, dtc, dAc, Bc, Cc = inp
        a_cum = jnp.cumsum(dAc, axis=1)
        seg = a_cum[:, :, None] - a_cum[:, None, :]
        decay = jnp.exp(jnp.where(lower, seg, -jnp.inf))
        cb = jnp.einsum('btgn,bsgn->btsg', Cc, Bc).astype(jnp.float32)
        y_diag = jnp.einsum('btsg,btsgj,bsgj,bsgjp->btgjp', cb, decay, dtc, xc.astype(jnp.float32))
        y_off = jnp.einsum('btgn,bgjpn,btgj->btgjp', Cc.astype(jnp.float32), state, jnp.exp(a_cum))
        a_last = a_cum[:, -1]
        w_s = jnp.exp(a_last[:, None] - a_cum) * dtc
        new_state = state * jnp.exp(a_last)[..., None, None] + jnp.einsum(
            'bsgn,bsgj,bsgjp->bgjpn', Bc.astype(jnp.float32), w_s, xc.astype(jnp.float32))
        return new_state, y_diag + y_off

    _, ys = lax.scan(step, state0, xs)
    return ys.swapaxes(0, 1).reshape(Bsz, S, G, J, P)


def ssd_mixer(z, xbc, dt_raw, conv_w, conv_b, dt_bias, A_log, D_skip, norm_w):
    Bsz, S, _ = z.shape
    G, J, P, N = SSD_N_GROUPS, SSD_HEADS_PER_GROUP, SSD_HEAD_DIM, SSD_D_STATE
    xbc = jax.nn.silu(centred_depthwise_conv(xbc, conv_w, conv_b))
    xs, Bm, Cm = split_columns(xbc, (SSD_D_INNER, SSD_BC_WIDTH, SSD_BC_WIDTH))
    xh = xs.reshape(Bsz, S, G, J, P)
    Bm = Bm.reshape(Bsz, S, G, N)
    Cm = Cm.reshape(Bsz, S, G, N)
    dt = jax.nn.softplus(dt_raw.reshape(Bsz, S, 2, SSD_N_HEADS).astype(jnp.float32)
                         + dt_bias.astype(jnp.float32))
    A = -jnp.exp(A_log.astype(jnp.float32))
    flip = lambda t: jnp.flip(t, axis=1)
    y_fwd = ssd_chunked_scan(xh, dt[:, :, 0].reshape(Bsz, S, G, J), A[0].reshape(G, J), Bm, Cm)
    y_bwd = flip(ssd_chunked_scan(flip(xh), flip(dt[:, :, 1]).reshape(Bsz, S, G, J),
                                  A[1].reshape(G, J), flip(Bm), flip(Cm)))
    y = y_fwd + y_bwd + D_skip.astype(jnp.float32).reshape(G, J)[..., None] * xh.astype(jnp.float32)
    y = y.reshape(Bsz, S, SSD_D_INNER)
    y = rms_norm(y * jax.nn.silu(z.astype(jnp.float32)), norm_w)
    return y.astype(z.dtype)


def dilated_window_attention(q, k, v, window, dilation):
    Bsz, S, H, Dh = q.shape
    half = window // (2 * dilation)
    n = S // dilation
    nb = -(-n // ATT_BLOCK)
    n_pad = nb * ATT_BLOCK
    kw = ATT_BLOCK + 2 * half
    sub = lambda t: t.reshape(Bsz, n, dilation, H, Dh)
    qs = jnp.pad(sub(q), ((0, 0), (0, n_pad - n), (0, 0), (0, 0), (0, 0)))
    qs = qs.reshape(Bsz, nb, ATT_BLOCK, dilation, H, Dh)
    kv_pad = ((0, 0), (half, half + n_pad - n), (0, 0), (0, 0), (0, 0))
    kp = jnp.pad(sub(k), kv_pad)
    vp = jnp.pad(sub(v), kv_pad)
    idx = jnp.arange(nb)[:, None] * ATT_BLOCK + jnp.arange(kw)[None, :]
    kb = kp[:, idx]
    vb = vp[:, idx]
    s = jnp.einsum('bjqrhd,bjkrhd->bjrhqk', qs, kb).astype(jnp.float32)
    qpos = jnp.arange(nb)[:, None] * ATT_BLOCK + jnp.arange(ATT_BLOCK)[None, :]
    kpos = idx - half
    valid = ((jnp.abs(kpos[:, None, :] - qpos[:, :, None]) <= half)
             & (kpos[:, None, :] >= 0) & (kpos[:, None, :] < n))
    s = jnp.where(valid[None, :, None, None], s, NEG_BIG)
    m = jnp.max(s, axis=-1, keepdims=True)
    p = jnp.exp(s - m)
    l = jnp.sum(p, axis=-1, keepdims=True)
    o = jnp.einsum('bjrhqk,bjkrhd->bjrhqd', p, vb.astype(jnp.float32)) / l
    lse = (m + jnp.log(l))[..., 0]
    o = o.transpose(0, 1, 4, 2, 3, 5).reshape(Bsz, n_pad, dilation, H, Dh)[:, :n]
    lse = lse.transpose(0, 1, 4, 2, 3).reshape(Bsz, n_pad, dilation, H)[:, :n]
    return o.reshape(Bsz, S, H, Dh), lse.reshape(Bsz, S, H)


def dilated_attention_mixer(qkv):
    Bsz, S, _ = qkv.shape
    q, k, v = [t.reshape(Bsz, S, ATT_N_HEADS, ATT_HEAD_DIM)
               for t in split_columns(qkv, (ATT_WIDTH, ATT_WIDTH, ATT_WIDTH))]
    q = partial_rotary(q) * (ATT_HEAD_DIM ** -0.5)
    k = partial_rotary(k)
    outs, lses = [], []
    for g in range(ATT_N_GROUPS):
        window, dilation = ATT_PATTERNS[g]
        sl = slice(g * ATT_HEADS_PER_GROUP, (g + 1) * ATT_HEADS_PER_GROUP)
        o, lse = dilated_window_attention(q[:, :, sl], k[:, :, sl], v[:, :, sl], window, dilation)
        outs.append(o)
        lses.append(lse)
    wts = jax.nn.softmax(jnp.stack(lses, axis=0), axis=0)
    o = jnp.sum(wts[..., None] * jnp.stack(outs, axis=0), axis=0)
    return o.reshape(Bsz, S, ATT_OUT_WIDTH).astype(qkv.dtype)


def setup_inputs(seed: int = 0) -> dict:
    key = jax.random.key(seed)
    ks = jax.random.split(key, 20)
    f32 = jnp.float32
    nrm = lambda k, shape, scale: jax.random.normal(k, shape, f32) * scale
    gain = lambda k, n: 1.0 + 0.02 * jax.random.normal(k, (DEPTH, n), f32)
    dt0 = jnp.exp(jax.random.uniform(ks[5], (DEPTH, 2, SSD_N_HEADS), f32)
                  * (np.log(0.1) - np.log(0.001)) + np.log(0.001))
    dt_bias = dt0 + jnp.log(-jnp.expm1(-dt0))
    A_log = jnp.log(jax.random.uniform(ks[6], (DEPTH, 2, SSD_N_HEADS), f32, 1.0, 16.0))
    return {
        "x": jax.random.normal(ks[0], (BATCH, SEQ, D_MODEL), f32),
        "norm_mix_pre": gain(ks[1], D_MODEL),
        "w_in": nrm(ks[2], (DEPTH, D_MODEL, IN_COLS), D_MODEL ** -0.5),
        "ssd_conv_w": nrm(ks[3], (DEPTH, SSD_CONV_WIDTH, SSD_CONV_CH), SSD_CONV_WIDTH ** -0.5),
        "ssd_conv_b": nrm(ks[4], (DEPTH, SSD_CONV_CH), 0.02),
        "ssd_dt_bias": dt_bias,
        "ssd_A_log": A_log,
        "ssd_D": 1.0 + 0.1 * jax.random.normal(ks[7], (DEPTH, SSD_N_HEADS), f32),
        "ssd_norm_w": gain(ks[8], SSD_D_INNER),
        "w_ssd_branch": nrm(ks[9], (DEPTH, SSD_D_INNER, D_MODEL), SSD_D_INNER ** -0.5),
        "w_attn_branch": nrm(ks[10], (DEPTH, ATT_OUT_WIDTH, D_MODEL), ATT_OUT_WIDTH ** -0.5),
        "w_out": nrm(ks[11], (DEPTH, D_MODEL, D_MODEL), D_MODEL ** -0.5),
        "norm_mix_post": gain(ks[12], D_MODEL),
        "norm_ffn_pre": gain(ks[13], D_MODEL),
        "w_ffn_in": nrm(ks[14], (DEPTH, D_MODEL, 2 * FFN_HIDDEN), D_MODEL ** -0.5),
        "w_ffn_down": nrm(ks[15], (DEPTH, FFN_HIDDEN, D_MODEL), FFN_HIDDEN ** -0.5),
        "norm_ffn_post": gain(ks[16], D_MODEL),
    }


def reference(x, norm_mix_pre, w_in, ssd_conv_w, ssd_conv_b, ssd_dt_bias, ssd_A_log, ssd_D,
              ssd_norm_w, w_ssd_branch, w_attn_branch, w_out, norm_mix_post, norm_ffn_pre,
              w_ffn_in, w_ffn_down, norm_ffn_post):
    for l in range(DEPTH):
        h = rms_norm(x, norm_mix_pre[l])
        proj = jnp.einsum('bsd,dc->bsc', h, w_in[l])
        z, xbc, dt_raw, qkv, gates = split_columns(proj, IN_SIZES)
        y_ssd = ssd_mixer(z, xbc, dt_raw, ssd_conv_w[l], ssd_conv_b[l], ssd_dt_bias[l],
                          ssd_A_log[l], ssd_D[l], ssd_norm_w[l])
        y_att = dilated_attention_mixer(qkv)
        g_ssd, g_att = split_columns(jax.nn.sigmoid(gates), (D_MODEL, D_MODEL))
        merged = (g_ssd * jnp.einsum('bsc,cd->bsd', y_ssd, w_ssd_branch[l])
                  + g_att * jnp.einsum('bsc,cd->bsd', y_att, w_attn_branch[l]))
        mix = jnp.einsum('bsd,de->bse', merged, w_out[l])
        x = x + rms_norm(mix, norm_mix_post[l]).astype(x.dtype)
        h = rms_norm(x, norm_ffn_pre[l])
        gu = jnp.einsum('bsd,df->bsf', h, w_ffn_in[l])
        gt, up = split_columns(gu, (FFN_HIDDEN, FFN_HIDDEN))
        y = jnp.einsum('bsf,fd->bsd', jax.nn.silu(gt) * up, w_ffn_down[l])
        x = x + rms_norm(y, norm_ffn_post[l]).astype(x.dtype)
    return x
```

```python
import functools

import jax
import jax.numpy as jnp
import numpy as np
from jax import lax
from jax.experimental import pallas as pl
from jax.experimental.pallas import tpu as pltpu

F32 = jnp.float32
BF16 = jnp.bfloat16

NORM_EPS = 1e-6
D_MODEL = 1024
SSD_D_INNER = 2048
SSD_HEAD_DIM = 64
SSD_N_GROUPS = 8
SSD_HEADS_PER_GROUP = 4
SSD_N_HEADS = SSD_N_GROUPS * SSD_HEADS_PER_GROUP
SSD_D_STATE = 128
SSD_GROUP_WIDTH = SSD_HEADS_PER_GROUP * SSD_HEAD_DIM
SSD_BC_WIDTH = SSD_N_GROUPS * SSD_D_STATE
SSD_CONV_WIDTH = 5
SSD_CHUNK = 128
ATT_HEAD_DIM = 64
ATT_HEADS_PER_GROUP = 8
ATT_PATTERNS = ((128, 1), (512, 4), (2048, 16))
ATT_DILS = tuple(d for _, d in ATT_PATTERNS)
ATT_HALF = 64
assert all(w // (2 * d) == ATT_HALF for w, d in ATT_PATTERNS)
ATT_GROUP_WIDTH = ATT_HEADS_PER_GROUP * ATT_HEAD_DIM
ATT_WIDTH = len(ATT_PATTERNS) * ATT_GROUP_WIDTH
ROPE_THETA = 500000.0
ROPE_DIM = ATT_HEAD_DIM // 4
NEG_BIG = -1e30
FFN_HIDDEN = 2816
LANES = 128
CONV_HALO = 16
ATT_BQ = 128
ATT_KW = ATT_BQ + 2 * ATT_HALF
VMEM_LIMIT = 56 * 1024 * 1024


def _cparams(n_axes):
    return pltpu.CompilerParams(
        dimension_semantics=("parallel",) * n_axes, vmem_limit_bytes=VMEM_LIMIT)


def _rms_scale(x, w):
    ms = jnp.mean(x * x, axis=-1, keepdims=True)
    return x * lax.rsqrt(ms + NORM_EPS) * w


def _sigmoid(a):
    return 1.0 / (1.0 + jnp.exp(-a))


def _softplus(a):
    return jnp.maximum(a, 0.0) + jnp.log(1.0 + jnp.exp(-jnp.abs(a)))


def _dot(a, b):
    return jnp.dot(a, b, preferred_element_type=F32)


def _dot_nt(a, b):
    return lax.dot_general(a, b, (((1,), (1,)), ((), ())), preferred_element_type=F32)


def _dot_tn(a, b):
    return lax.dot_general(a, b, (((0,), (0,)), ((), ())), preferred_element_type=F32)


def _full(shape):
    n = len(shape)
    return pl.BlockSpec(shape, lambda *_: (0,) * n)


def _head_selectors(rows):
    lo = (lax.broadcasted_iota(jnp.int32, (rows, LANES), 1) < LANES // 2).astype(F32)
    return lo.astype(BF16), (1.0 - lo).astype(BF16)


def _proj_gate_kernel(x_ref, nw_ref, wz_ref, wg_ref, wdt_ref, bias_ref, alog_ref,
                      zs_ref, gs_ref, dt_ref, da_ref, hn_ref, *, tn):
    hn_ref[...] = _rms_scale(x_ref[0], nw_ref[...]).astype(BF16)
    for c in range(wz_ref.shape[1] // tn):
        sl = slice(c * tn, (c + 1) * tn)
        a = _dot(hn_ref[...], wz_ref[:, sl])
        zs_ref[0, :, sl] = (a * _sigmoid(a)).astype(BF16)
    for c in range(wg_ref.shape[1] // tn):
        sl = slice(c * tn, (c + 1) * tn)
        a = _dot(hn_ref[...], wg_ref[:, sl])
        gs_ref[0, :, sl] = _sigmoid(a).astype(BF16)
    d = _dot(hn_ref[...], wdt_ref[...])
    n_dt = dt_ref.shape[1]
    dt = _softplus(d.T[:n_dt, :] + bias_ref[...])
    dt_ref[0] = dt
    da_ref[0] = dt * (-jnp.exp(alog_ref[...]))


def _proj_gate(x, nw, wz, wg, wdt, bias_col, alog_col, *, tm=512, tn=256):
    B, S, D = x.shape
    n_dt = bias_col.shape[0]
    grid = (B, S // tm)
    tok = lambda w: pl.BlockSpec((1, tm, w), lambda b, i: (b, i, 0))
    return pl.pallas_call(
        functools.partial(_proj_gate_kernel, tn=tn),
        grid=grid,
        in_specs=[tok(D), _full(nw.shape), _full(wz.shape), _full(wg.shape), _full(wdt.shape),
                  _full(bias_col.shape), _full(alog_col.shape)],
        out_specs=[tok(wz.shape[1]), tok(wg.shape[1]),
                   pl.BlockSpec((1, n_dt, tm), lambda b, i: (b, 0, i)),
                   pl.BlockSpec((1, n_dt, tm), lambda b, i: (b, 0, i))],
        out_shape=[jax.ShapeDtypeStruct((B, S, wz.shape[1]), BF16),
                   jax.ShapeDtypeStruct((B, S, wg.shape[1]), BF16),
                   jax.ShapeDtypeStruct((B, n_dt, S), F32),
                   jax.ShapeDtypeStruct((B, n_dt, S), F32)],
        scratch_shapes=[pltpu.VMEM((tm, D), BF16)],
        compiler_params=_cparams(2),
        name="proj_gate",
    )(x, nw, wz, wg, wdt, bias_col, alog_col)


def _proj_xbc_kernel(xp_ref, x_ref, xn_ref, nw_ref, wx_ref, wb_ref, wc_ref, cw_ref,
                     xs_ref, bm_ref, cm_ref, hn_ref, f_ref, *, tm):
    it = pl.program_id(1)
    H = CONV_HALO
    nw = nw_ref[...]
    keep_p = jnp.where(it > 0, 1.0, 0.0)
    keep_n = jnp.where(it < pl.num_programs(1) - 1, 1.0, 0.0)
    hn_ref[0:H] = (_rms_scale(xp_ref[0], nw) * keep_p).astype(BF16)
    hn_ref[H:H + tm] = _rms_scale(x_ref[0], nw).astype(BF16)
    hn_ref[H + tm:] = (_rms_scale(xn_ref[0], nw) * keep_n).astype(BF16)
    W = SSD_GROUP_WIDTH
    half = SSD_CONV_WIDTH // 2

    def conv_silu(acc_ext, col0):
        f_ref[...] = acc_ext
        y = cw_ref[SSD_CONV_WIDTH:SSD_CONV_WIDTH + 1, col0:col0 + W]
        for k in range(SSD_CONV_WIDTH):
            r0 = H - half + k
            y = y + cw_ref[k:k + 1, col0:col0 + W] * f_ref[r0:r0 + tm, :]
        return (y * _sigmoid(y)).astype(BF16)

    for g in range(SSD_N_GROUPS):
        a = _dot(hn_ref[...], wx_ref[:, g * W:(g + 1) * W])
        xs_ref[0, g] = conv_silu(a, g * W)
    for ref, w_ref, base in ((bm_ref, wb_ref, SSD_D_INNER), (cm_ref, wc_ref, SSD_D_INNER + SSD_BC_WIDTH)):
        for p in range(SSD_BC_WIDTH // W):
            a = _dot(hn_ref[...], w_ref[:, p * W:(p + 1) * W])
            y = conv_silu(a, base + p * W)
            ref[0, 2 * p] = y[:, :SSD_D_STATE]
            ref[0, 2 * p + 1] = y[:, SSD_D_STATE:]


def _proj_xbc(x, nw, wx, wb, wc, cw8, *, tm=512):
    B, S, D = x.shape
    H = CONV_HALO
    G = SSD_N_GROUPS
    nh = S // H
    r = tm // H
    grid = (B, S // tm)
    return pl.pallas_call(
        functools.partial(_proj_xbc_kernel, tm=tm),
        grid=grid,
        in_specs=[pl.BlockSpec((1, H, D), lambda b, i: (b, jnp.maximum(i * r - 1, 0), 0)),
                  pl.BlockSpec((1, tm, D), lambda b, i: (b, i, 0)),
                  pl.BlockSpec((1, H, D), lambda b, i: (b, jnp.minimum((i + 1) * r, nh - 1), 0)),
                  _full(nw.shape), _full(wx.shape), _full(wb.shape), _full(wc.shape),
                  _full(cw8.shape)],
        out_specs=[pl.BlockSpec((1, G, tm, SSD_GROUP_WIDTH), lambda b, i: (b, 0, i, 0)),
                   pl.BlockSpec((1, G, tm, SSD_D_STATE), lambda b, i: (b, 0, i, 0)),
                   pl.BlockSpec((1, G, tm, SSD_D_STATE), lambda b, i: (b, 0, i, 0))],
        out_shape=[jax.ShapeDtypeStruct((B, G, S, SSD_GROUP_WIDTH), BF16),
                   jax.ShapeDtypeStruct((B, G, S, SSD_D_STATE), BF16),
                   jax.ShapeDtypeStruct((B, G, S, SSD_D_STATE), BF16)],
        scratch_shapes=[pltpu.VMEM((tm + 2 * H, D), BF16),
                        pltpu.VMEM((tm + 2 * H, SSD_GROUP_WIDTH), F32)],
        compiler_params=_cparams(2),
        name="proj_xbc",
    )(x, x, x, nw, wx, wb, wc, cw8)


def _proj_qkv_kernel(x_ref, nw_ref, w_ref, cos_ref, sa_ref, sb_ref, *rest, tm):
    outs, (hn_ref, f_ref) = rest[:9], rest[9:]
    hn_ref[...] = _rms_scale(x_ref[0], nw_ref[...]).astype(BF16)
    W = ATT_GROUP_WIDTH
    for typ in range(3):
        for g, d in enumerate(ATT_DILS):
            c = typ * len(ATT_DILS) + g
            a = _dot(hn_ref[...], w_ref[:, c * W:(c + 1) * W])
            if typ < 2:
                parts = []
                for p in range(W // LANES):
                    ch = a[:, p * LANES:(p + 1) * LANES]
                    parts.append(ch * cos_ref[...]
                                 + pltpu.roll(ch, ROPE_DIM // 2, 1) * sa_ref[...]
                                 + pltpu.roll(ch, LANES - ROPE_DIM // 2, 1) * sb_ref[...])
                a = jnp.concatenate(parts, axis=1)
            if typ == 0:
                a = a * (ATT_HEAD_DIM ** -0.5)
            o_ref = outs[c]
            if d == 1:
                o_ref[0, 0] = a.astype(BF16)
            else:
                for p in range(W // LANES):
                    f_ref[p] = a[:, p * LANES:(p + 1) * LANES]
                for r in range(d):
                    for p in range(W // LANES):
                        o_ref[0, r, :, p * LANES:(p + 1) * LANES] = (
                            f_ref[p, pl.ds(r, tm // d, stride=d), :].astype(BF16))


def _proj_qkv(x, nw, wqkv, cos_t, sa_t, sb_t, *, tm=512):
    B, S, D = x.shape
    W = ATT_GROUP_WIDTH
    grid = (B, S // tm)
    tab = pl.BlockSpec((tm, LANES), lambda b, i: (i, 0))
    out_specs, out_shape = [], []
    for _ in range(3):
        for d in ATT_DILS:
            out_specs.append(pl.BlockSpec((1, d, tm // d, W), lambda b, i: (b, 0, i, 0)))
            out_shape.append(jax.ShapeDtypeStruct((B, d, S // d, W), BF16))
    return pl.pallas_call(
        functools.partial(_proj_qkv_kernel, tm=tm),
        grid=grid,
        in_specs=[pl.BlockSpec((1, tm, D), lambda b, i: (b, i, 0)), _full(nw.shape),
                  _full(wqkv.shape), tab, tab, tab],
        out_specs=out_specs,
        out_shape=out_shape,
        scratch_shapes=[pltpu.VMEM((tm, D), BF16), pltpu.VMEM((W // LANES, tm, LANES), F32)],
        compiler_params=_cparams(2),
        name="proj_qkv",
    )(x, nw, wqkv, cos_t, sa_t, sb_t)


def _ssd_scan_kernel(xs_ref, bm_ref, cm_ref, dt_ref, da_ref, dsk_ref, y_ref,
                     r_ref, qr_ref, qc_ref, st_ref, yacc_ref, *, S):
    L = SSD_CHUNK
    nc = S // L
    J = SSD_HEADS_PER_GROUP
    da = da_ref[0]
    dt = dt_ref[0]
    lane_in_chunk = lax.broadcasted_iota(jnp.int32, da.shape, 1) & (L - 1)
    pre, suf = da, da
    k = 1
    while k < L:
        pre = pre + jnp.where(lane_in_chunk >= k, pltpu.roll(pre, k, 1), 0.0)
        suf = suf + jnp.where(lane_in_chunk < L - k, pltpu.roll(suf, S - k, 1), 0.0)
        k *= 2
    tot = pre + suf - da
    is_fwd = lax.broadcasted_iota(jnp.int32, da.shape, 0) < J
    a = jnp.where(is_fwd, pre, suf)
    r_ref[0:2 * J] = a
    r_ref[2 * J:4 * J] = dt
    r_ref[4 * J:6 * J] = jnp.exp(tot)
    qr_ref[0:2 * J] = a
    qr_ref[2 * J:4 * J] = jnp.exp(a)
    qr_ref[4 * J:6 * J] = jnp.exp(tot - a) * dt
    qr_ref[6 * J:] = jnp.zeros((qr_ref.shape[0] - 6 * J, S), F32)
    for c in range(nc):
        qc_ref[c * L:(c + 1) * L, :] = qr_ref[:, c * L:(c + 1) * L].T
    st_ref[...] = jnp.zeros(st_ref.shape, F32)

    lo = lax.broadcasted_iota(jnp.int32, (L, LANES), 1) < SSD_HEAD_DIM
    lo_row = lo[0:1]
    sel = _head_selectors(L)
    ti =lax.broadcasted_iota(jnp.int32, (L, L), 0)
    si = lax.broadcasted_iota(jnp.int32, (L, L), 1)
    causal = (ti >= si, si >= ti)
    dsk = dsk_ref[0]

    def chunk(c, d):
        t0 = pl.multiple_of(c * L, L)
        xc = xs_ref[0, 0, pl.ds(t0, L), :]
        bc = bm_ref[0, 0, pl.ds(t0, L), :]
        cc = cm_ref[0, 0, pl.ds(t0, L), :]
        cb = _dot_nt(cc, bc)
        qc = qc_ref[pl.ds(t0, L), :]
        rr = r_ref[:, pl.ds(t0, L)]
        ms = []
        for j in range(J):
            r = J * d + j
            seg = qc[:, r:r + 1] - rr[r:r + 1, :]
            dec = jnp.exp(jnp.where(causal[d], seg, -jnp.inf))
            ms.append((cb * dec * rr[2 * J + r:2 * J + r + 1, :]).astype(BF16))
        yds = []
        for p in range(J // 2):
            xp = xc[:, p * LANES:(p + 1) * LANES]
            lhs = jnp.concatenate([ms[2 * p], ms[2 * p + 1]], axis=1)
            rhs = jnp.concatenate([xp * sel[0], xp * sel[1]], axis=0)
            yds.append(_dot(lhs, rhs))
        yd = jnp.concatenate(yds, axis=1)

        def colscale(base):
            parts = [jnp.where(lo, qc[:, base + 2 * p:base + 2 * p + 1],
                               qc[:, base + 2 * p + 1:base + 2 * p + 2]) for p in range(J // 2)]
            return jnp.concatenate(parts, axis=1)

        st = st_ref[d]
        yo = _dot(cc, st.astype(BF16)) * colscale(2 * J + J * d)
        xf = xc.astype(F32)
        xw = (xf * colscale(4 * J + J * d)).astype(BF16)
        ds = _dot_tn(bc, xw)
        et = rr[4 * J + J * d:4 * J + J * d + J, :]
        drow = jnp.concatenate([jnp.where(lo_row, et[2 * p:2 * p + 1], et[2 * p + 1:2 * p + 2])
                                for p in range(J // 2)], axis=1)
        st_ref[d] = st * drow + ds
        return t0, yd + yo, xf

    def first_touch(i, carry):
        for c, d in ((i, 0), (nc - 1 - i, 1)):
            t0, y, _ = chunk(c, d)
            yacc_ref[pl.ds(t0, L), :] = y
        return carry

    def second_touch(i, carry):
        for c, d in ((i, 0), (nc - 1 - i, 1)):
            t0, y, xf = chunk(c, d)
            y_ref[0, pl.ds(t0, L), :] = (yacc_ref[pl.ds(t0, L), :] + y + xf * dsk).astype(BF16)
        return carry

    lax.fori_loop(0, nc // 2, first_touch, 0)
    lax.fori_loop(nc // 2, nc, second_touch, 0)


def _ssd_scan(xs, bm, cm, dt_t, da_t, dsk):
    B, G, S, W = xs.shape
    N = SSD_D_STATE
    J2 = 2 * SSD_HEADS_PER_GROUP
    assert (S // SSD_CHUNK) % 2 == 0
    grp = lambda w: pl.BlockSpec((1, 1, S, w), lambda b, g: (b, g, 0, 0))
    return pl.pallas_call(
        functools.partial(_ssd_scan_kernel, S=S),
        grid=(B, G),
        in_specs=[grp(W), grp(N), grp(N),
                  pl.BlockSpec((1, J2, S), lambda b, g: (b, g, 0)),
                  pl.BlockSpec((1, J2, S), lambda b, g: (b, g, 0)),
                  pl.BlockSpec((1, 1, W), lambda b, g: (g, 0, 0))],
        out_specs=pl.BlockSpec((1, S, W), lambda b, g: (b, 0, g)),
        out_shape=jax.ShapeDtypeStruct((B, S, G * W), BF16),
        scratch_shapes=[pltpu.VMEM((3 * J2, S), F32), pltpu.VMEM((LANES, S), F32),
                        pltpu.VMEM((S, LANES), F32), pltpu.VMEM((2, N, W), F32),
                        pltpu.VMEM((S, W), F32)],
        compiler_params=_cparams(2),
        name="ssd_scan",
    )(xs, bm, cm, dt_t, da_t, dsk)


def _dil_attn_kernel(*refs, S):
    qkv_refs, bias_ref, o_ref, scr = refs[:9], refs[9], refs[10], refs[11:]
    BQ, KW, HD = ATT_BQ, ATT_KW, ATT_HEAD_DIM
    lo_q = lax.broadcasted_iota(jnp.int32, (BQ, LANES), 1) < HD
    sel_q = _head_selectors(BQ)
    sel_k = _head_selectors(KW)

    def attend(qb, kwin, vwin, bias):
        ps, mx = [], []
        for h in range(2):
            s = _dot_nt(qb * sel_q[h], kwin) + bias
            m = jnp.max(s, axis=-1, keepdims=True)
            ps.append(jnp.exp(s - m).astype(BF16))
            mx.append(m)
        va = jnp.concatenate([vwin * sel_k[0], sel_k[0]], axis=1)
        vb = jnp.concatenate([vwin * sel_k[1], sel_k[1]], axis=1)
        accl = _dot(jnp.concatenate(ps, axis=1), jnp.concatenate([va, vb], axis=0))
        den = accl[:, LANES:]
        o = accl[:, :LANES] / den
        lse = jnp.where(lo_q, mx[0], mx[1]) + jnp.log(den)
        return o, lse

    def block(q_ref, k_ref, v_ref, n, it):
        nblk = n // BQ
        r = it // nblk
        qs = pl.multiple_of((it % nblk) * BQ, BQ)
        ks = pl.multiple_of(jnp.clip(qs - ATT_HALF, 0, n - KW), ATT_HALF)
        variant = (qs - ks) // ATT_HALF
        o, lse = attend(q_ref[0, r, pl.ds(qs, BQ), :], k_ref[0, r, pl.ds(ks, KW), :],
                        v_ref[0, r, pl.ds(ks, KW), :], bias_ref[variant])
        return r, qs, o, lse

    for g in range(1, len(ATT_DILS)):
        d = ATT_DILS[g]
        q_ref, k_ref, v_ref = qkv_refs[g], qkv_refs[3 + g], qkv_refs[6 + g]
        acc_ref, lse_ref = scr[2 * (g - 1)], scr[2 * (g - 1) + 1]

        def dil_body(it, carry, d=d, q_ref=q_ref, k_ref=k_ref, v_ref=v_ref,
                     acc_ref=acc_ref, lse_ref=lse_ref):
            r, qs, o, lse = block(q_ref, k_ref, v_ref, S // d, it)
            acc_ref[pl.ds(r + d * qs, BQ, stride=d), :] = o
            lse_ref[pl.ds(r + d * qs, BQ, stride=d), :] = lse
            return carry

        lax.fori_loop(0, S // BQ, dil_body, 0)

    def merge_body(it, carry):
        _, qs, o, lse = block(qkv_refs[0], qkv_refs[3], qkv_refs[6], S, it)
        os_, ls_ = [o], [lse]
        for g in range(1, len(ATT_DILS)):
            os_.append(scr[2 * (g - 1)][pl.ds(qs, BQ), :])
            ls_.append(scr[2 * (g - 1) + 1][pl.ds(qs, BQ), :])
        m = functools.reduce(jnp.maximum, ls_)
        ws = [jnp.exp(l - m) for l in ls_]
        num = functools.reduce(lambda a, b: a + b, [w * v for w, v in zip(ws, os_)])
        den = functools.reduce(lambda a, b: a + b, ws)
        o_ref[0, pl.ds(qs, BQ), :] = (num / den).astype(BF16)
        return carry

    lax.fori_loop(0, S // BQ, merge_body, 0)


def _dil_attn(qkv, bias):
    B = qkv[0].shape[0]
    S = qkv[0].shape[2]
    W = ATT_GROUP_WIDTH
    in_specs = []
    for t in range(3):
        for d in ATT_DILS:
            in_specs.append(pl.BlockSpec((1, d, S // d, LANES), lambda b, p: (b, 0, 0, p)))
    in_specs.append(_full(bias.shape))
    n_scr = 2 * (len(ATT_DILS) - 1)
    return pl.pallas_call(
        functools.partial(_dil_attn_kernel, S=S),
        grid=(B, W // LANES),
        in_specs=in_specs,
        out_specs=pl.BlockSpec((1, S, LANES), lambda b, p: (b, 0, p)),
        out_shape=jax.ShapeDtypeStruct((B, S, W), BF16),
        scratch_shapes=[pltpu.VMEM((S, LANES), F32)] * n_scr,
        compiler_params=_cparams(2),
        name="dil_attn",
    )(*qkv, bias)


def _mix_merge_kernel(x_ref, ys_ref, zs_ref, ya_ref, gs_ref, nws_ref, ws_ref, wa_ref, wo_ref,
                      nwo_ref, o_ref):
    u = ys_ref[...].astype(F32) * zs_ref[...].astype(F32)
    un = _rms_scale(u, nws_ref[...]).astype(BF16)
    a = _dot(un, ws_ref[...])
    b = _dot(ya_ref[...], wa_ref[...])
    gs = gs_ref[...].astype(F32)
    merged = gs[:, :D_MODEL] * a + gs[:, D_MODEL:] * b
    mix = _dot(merged.astype(BF16), wo_ref[...])
    o_ref[...] = x_ref[...] + _rms_scale(mix, nwo_ref[...])


def _mix_merge(x2, ys, zs, ya, gs, nws, ws, wa, wo, nwo, *, tm=512):
    T, D = x2.shape
    tok = lambda w: pl.BlockSpec((tm, w), lambda i: (i, 0))
    return pl.pallas_call(
        _mix_merge_kernel,
        grid=(T // tm,),
        in_specs=[tok(D), tok(ys.shape[1]), tok(zs.shape[1]), tok(ya.shape[1]), tok(gs.shape[1]),
                  _full(nws.shape), _full(ws.shape), _full(wa.shape), _full(wo.shape),
                  _full(nwo.shape)],
        out_specs=tok(D),
        out_shape=jax.ShapeDtypeStruct((T, D), F32),
        compiler_params=_cparams(1),
        name="mix_merge",
    )(x2, ys, zs, ya, gs, nws, ws, wa, wo, nwo)


def _ffn_kernel(x_ref, nwi_ref, wg_ref, wu_ref, wd_ref, nwo_ref, o_ref, hn_ref, act_ref, *, th):
    x = x_ref[...]
    hn_ref[...] = _rms_scale(x, nwi_ref[...]).astype(BF16)
    for c in range(wg_ref.shape[1] // th):
        sl = slice(c * th, (c + 1) * th)
        gt = _dot(hn_ref[...], wg_ref[:, sl])
        up = _dot(hn_ref[...], wu_ref[:, sl])
        act_ref[:, sl] = (gt * _sigmoid(gt) * up).astype(BF16)
    y = _dot(act_ref[...], wd_ref[...])
    o_ref[...] = x + _rms_scale(y, nwo_ref[...])


def _ffn(x2, nwi, wg, wu, wd, nwo, *, tm=512, th=256):
    T, D = x2.shape
    Hf = wg.shape[1]
    tok = pl.BlockSpec((tm, D), lambda i: (i, 0))
    return pl.pallas_call(
        functools.partial(_ffn_kernel, th=th),
        grid=(T // tm,),
        in_specs=[tok, _full(nwi.shape), _full(wg.shape), _full(wu.shape), _full(wd.shape),
                  _full(nwo.shape)],
        out_specs=tok,
        out_shape=jax.ShapeDtypeStruct((T, D), F32),
        scratch_shapes=[pltpu.VMEM((tm, D), BF16), pltpu.VMEM((tm, Hf), BF16)],
        compiler_params=_cparams(1),
        name="ffn",
    )(x2, nwi, wg, wu, wd, nwo)


def _rope_tables(S):
    pos = jnp.arange(S, dtype=F32)
    inv_freq = ROPE_THETA ** (-jnp.arange(0, ROPE_DIM, 2, dtype=F32) / ROPE_DIM)
    ang = pos[:, None] * inv_freq[None, :]
    h = ROPE_DIM // 2
    cos, sin = jnp.cos(ang), jnp.sin(ang)
    pad = jnp.zeros((S, ATT_HEAD_DIM - ROPE_DIM), F32)
    zh = jnp.zeros((S, h), F32)
    cos_h = jnp.concatenate([cos, cos, pad + 1.0], axis=1)
    sa_h = jnp.concatenate([zh, sin, pad], axis=1)
    sb_h = jnp.concatenate([-sin, zh, pad], axis=1)
    rep = LANES // ATT_HEAD_DIM
    return tuple(jnp.tile(t, (1, rep)) for t in (cos_h, sa_h, sb_h))


def _attn_bias():
    i = np.arange(ATT_BQ)[:, None]
    j = np.arange(ATT_KW)[None, :]
    v = np.arange(3)[:, None, None]
    ok = np.abs(j - i - ATT_HALF * v) <= ATT_HALF
    return jnp.asarray(np.where(ok, 0.0, NEG_BIG), dtype=F32)


def _layer(x, nmp, w_in, conv_w, conv_b, dt_bias, a_log, d_skip, ssd_nw, w_ssd, w_att, w_out,
           nmpost, nfpre, w_fin, w_fdown, nfpost):
    B, S, D = x.shape
    G, J = SSD_N_GROUPS, SSD_HEADS_PER_GROUP
    row = lambda v: v.reshape(1, -1).astype(F32)
    c0 = SSD_D_INNER
    c1 = c0 + SSD_D_INNER + 2 * SSD_BC_WIDTH
    c2 = c1 + 2 * SSD_N_HEADS
    c3 = c2 + 3 * ATT_WIDTH
    wz = w_in[:, :c0].astype(BF16)
    wx = w_in[:, c0:c0 + SSD_D_INNER].astype(BF16)
    wb = w_in[:, c0 + SSD_D_INNER:c0 + SSD_D_INNER + SSD_BC_WIDTH].astype(BF16)
    wc = w_in[:, c0 + SSD_D_INNER + SSD_BC_WIDTH:c1].astype(BF16)
    perm = lambda v: v.reshape(v.shape[:-1] + (2, G, J)).swapaxes(-3, -2).reshape(v.shape[:-1] + (2 * G * J,))
    wdt = perm(w_in[:, c1:c2])
    wdt = jnp.concatenate([wdt, jnp.zeros((D, LANES - wdt.shape[1]), wdt.dtype)], axis=1).astype(BF16)
    wqkv = w_in[:, c2:c3].astype(BF16)
    wgt = w_in[:, c3:].astype(BF16)
    bias_col = perm(dt_bias.reshape(-1)).reshape(-1, 1).astype(F32)
    alog_col = perm(a_log.reshape(-1)).reshape(-1, 1).astype(F32)
    nw = row(nmp)

    zs, gs, dt_t, da_t = _proj_gate(x, nw, wz, wgt, wdt, bias_col, alog_col)
    cw8 = jnp.concatenate([conv_w, conv_b[None, :],
                           jnp.zeros((8 - SSD_CONV_WIDTH - 1, conv_w.shape[1]), F32)], axis=0)
    xs, bm, cm = _proj_xbc(x, nw, wx, wb, wc, cw8)
    qkv = _proj_qkv(x, nw, wqkv, *_rope_tables(S))

    dsk = jnp.repeat(d_skip.astype(F32), SSD_HEAD_DIM).reshape(G, 1, SSD_GROUP_WIDTH)
    y_ssd = _ssd_scan(xs, bm, cm, dt_t, da_t, dsk)
    y_att = _dil_attn(qkv, _attn_bias())

    T = B * S
    x1 = _mix_merge(x.reshape(T, D), y_ssd.reshape(T, -1), zs.reshape(T, -1), y_att.reshape(T, -1),
                    gs.reshape(T, -1), row(ssd_nw), w_ssd.astype(BF16), w_att.astype(BF16),
                    w_out.astype(BF16), row(nmpost))
    x2 = _ffn(x1, row(nfpre), w_fin[:, :FFN_HIDDEN].astype(BF16), w_fin[:, FFN_HIDDEN:].astype(BF16),
              w_fdown.astype(BF16), row(nfpost))
    return x2.reshape(B, S, D)


def kernel(x, norm_mix_pre, w_in, ssd_conv_w, ssd_conv_b, ssd_dt_bias, ssd_A_log, ssd_D, ssd_norm_w,
           w_ssd_branch, w_attn_branch, w_out, norm_mix_post, norm_ffn_pre, w_ffn_in, w_ffn_down,
           norm_ffn_post):
    for l in range(w_in.shape[0]):
        x = _layer(x, norm_mix_pre[l], w_in[l], ssd_conv_w[l], ssd_conv_b[l], ssd_dt_bias[l],
                   ssd_A_log[l], ssd_D[l], ssd_norm_w[l], w_ssd_branch[l], w_attn_branch[l], w_out[l],
                   norm_mix_post[l], norm_ffn_pre[l], w_ffn_in[l], w_ffn_down[l], norm_ffn_post[l])
    return x
```

```python
import functools

import jax
import jax.numpy as jnp
import numpy as np
from jax import lax
from jax.experimental import pallas as pl
from jax.experimental.pallas import tpu as pltpu

F32 = jnp.float32
BF16 = jnp.bfloat16

NORM_EPS = 1e-6
D_MODEL = 1024
SSD_D_INNER = 2048
SSD_HEAD_DIM = 64
SSD_N_GROUPS = 8
SSD_HEADS_PER_GROUP = 4
SSD_N_HEADS = SSD_N_GROUPS * SSD_HEADS_PER_GROUP
SSD_D_STATE = 128
SSD_GROUP_WIDTH = SSD_HEADS_PER_GROUP * SSD_HEAD_DIM
SSD_BC_WIDTH = SSD_N_GROUPS * SSD_D_STATE
SSD_CONV_WIDTH = 5
SSD_CHUNK = 128
SSD_UNROLL = 2
ATT_HEAD_DIM = 64
ATT_HEADS_PER_GROUP = 8
ATT_PATTERNS = ((128, 1), (512, 4), (2048, 16))
ATT_DILS = tuple(d for _, d in ATT_PATTERNS)
ATT_HALF = 64
assert all(w // (2 * d) == ATT_HALF for w, d in ATT_PATTERNS)
ATT_GROUP_WIDTH = ATT_HEADS_PER_GROUP * ATT_HEAD_DIM
ATT_WIDTH = len(ATT_PATTERNS) * ATT_GROUP_WIDTH
ROPE_THETA = 500000.0
ROPE_DIM = ATT_HEAD_DIM // 4
NEG_BIG = -1e30
LOG2E = 1.4426950408889634
FFN_HIDDEN = 2816
LANES = 128
SUBLANES = 8
CONV_HALO = 16
CONV_ROWS = 64
ATT_BQ = 128
ATT_KW = ATT_BQ + 2 * ATT_HALF
ATT_UNROLL = 8
VMEM_LIMIT = 56 * 1024 * 1024


def _cparams(n_axes):
    return pltpu.CompilerParams(
        dimension_semantics=("parallel",) * n_axes, vmem_limit_bytes=VMEM_LIMIT)


def _rms_scale(x, w):
    ms = jnp.mean(x * x, axis=-1, keepdims=True)
    return x * lax.rsqrt(ms + NORM_EPS) * w


def _sigmoid(a):
    return 1.0 / (1.0 + jnp.exp(-a))


def _softplus(a):
    return jnp.maximum(a, 0.0) + jnp.log(1.0 + jnp.exp(-jnp.abs(a)))


def _dot(a, b):
    return jnp.dot(a, b, preferred_element_type=F32)


def _dot_nt(a, b):
    return lax.dot_general(a, b, (((1,), (1,)), ((), ())), preferred_element_type=F32)


def _dot_tn(a, b):
    return lax.dot_general(a, b, (((0,), (0,)), ((), ())), preferred_element_type=F32)


def _full(shape):
    n = len(shape)
    return pl.BlockSpec(shape, lambda *_: (0,) * n)


def _head_selectors(rows):
    lo = (lax.broadcasted_iota(jnp.int32, (rows, LANES), 1) < LANES // 2).astype(F32)
    return lo.astype(BF16), (1.0 - lo).astype(BF16)


def _proj_gate_kernel(x_ref, nw_ref, wz_ref, wg_ref, wdt_ref, bias_ref, alog_ref,
                      zs_ref, gs_ref, dt_ref, da_ref, hn_ref, *, tn):
    hn_ref[...] = _rms_scale(x_ref[0], nw_ref[...]).astype(BF16)
    for c in range(wz_ref.shape[1] // tn):
        sl = slice(c * tn, (c + 1) * tn)
        a = _dot(hn_ref[...], wz_ref[:, sl])
        zs_ref[0, :, sl] = (a * _sigmoid(a)).astype(BF16)
    for c in range(wg_ref.shape[1] // tn):
        sl = slice(c * tn, (c + 1) * tn)
        a = _dot(hn_ref[...], wg_ref[:, sl])
        gs_ref[0, :, sl] = _sigmoid(a).astype(BF16)
    d = _dot(hn_ref[...], wdt_ref[...])
    n_dt = dt_ref.shape[1]
    dt = _softplus(d.T[:n_dt, :] + bias_ref[...])
    dt_ref[0] = dt
    da_ref[0] = dt * (-jnp.exp(alog_ref[...]))


def _proj_gate(x, nw, wz, wg, wdt, bias_col, alog_col, *, tm=512, tn=256):
    B, S, D = x.shape
    n_dt = bias_col.shape[0]
    grid = (B, S // tm)
    tok = lambda w: pl.BlockSpec((1, tm, w), lambda b, i: (b, i, 0))
    return pl.pallas_call(
        functools.partial(_proj_gate_kernel, tn=tn),
        grid=grid,
        in_specs=[tok(D), _full(nw.shape), _full(wz.shape), _full(wg.shape), _full(wdt.shape),
                  _full(bias_col.shape), _full(alog_col.shape)],
        out_specs=[tok(wz.shape[1]), tok(wg.shape[1]),
                   pl.BlockSpec((1, n_dt, tm), lambda b, i: (b, 0, i)),
                   pl.BlockSpec((1, n_dt, tm), lambda b, i: (b, 0, i))],
        out_shape=[jax.ShapeDtypeStruct((B, S, wz.shape[1]), BF16),
                   jax.ShapeDtypeStruct((B, S, wg.shape[1]), BF16),
                   jax.ShapeDtypeStruct((B, n_dt, S), F32),
                   jax.ShapeDtypeStruct((B, n_dt, S), F32)],
        scratch_shapes=[pltpu.VMEM((tm, D), BF16)],
        compiler_params=_cparams(2),
        name="proj_gate",
    )(x, nw, wz, wg, wdt, bias_col, alog_col)


def _proj_xbc_kernel(xp_ref, x_ref, xn_ref, nw_ref, wx_ref, wb_ref, wc_ref, cw_ref,
                     xs_ref, bm_ref, cm_ref, hn_ref, f_ref, *, tm):
    it = pl.program_id(1)
    H = CONV_HALO
    nw = nw_ref[...]
    keep_p = jnp.where(it > 0, 1.0, 0.0)
    keep_n = jnp.where(it < pl.num_programs(1) - 1, 1.0, 0.0)
    hn_ref[0:H] = (_rms_scale(xp_ref[0], nw) * keep_p).astype(BF16)
    hn_ref[H:H + tm] = _rms_scale(x_ref[0], nw).astype(BF16)
    hn_ref[H + tm:] = (_rms_scale(xn_ref[0], nw) * keep_n).astype(BF16)
    W = SSD_GROUP_WIDTH
    half = SSD_CONV_WIDTH // 2

    sub = lax.broadcasted_iota(jnp.int32, (1, SUBLANES, W), 1)

    def shifted(x3, s):
        xr = pltpu.roll(x3, s % SUBLANES, 1)
        if s > 0:
            return jnp.where(sub < s, jnp.concatenate([xr[-1:], xr[:-1]], axis=0), xr)
        return jnp.where(sub >= SUBLANES + s, jnp.concatenate([xr[1:], xr[:1]], axis=0), xr)

    def conv_silu(acc_ext, col0, store):
        f_ref[...] = acc_ext
        for rb in range(tm // CONV_ROWS):
            r0 = H + rb * CONV_ROWS
            blk = f_ref[r0 - SUBLANES:r0 + CONV_ROWS + SUBLANES, :]
            x3 = blk.reshape(blk.shape[0] // SUBLANES, SUBLANES, W)
            y = cw_ref[SSD_CONV_WIDTH:SSD_CONV_WIDTH + 1, col0:col0 + W]
            for k in range(SSD_CONV_WIDTH):
                sh = x3 if k == half else shifted(x3, half - k)
                sh = sh.reshape(blk.shape)[SUBLANES:SUBLANES + CONV_ROWS, :]
                y = y + cw_ref[k:k + 1, col0:col0 + W] * sh
            store(rb * CONV_ROWS, (y * _sigmoid(y)).astype(BF16))

    for g in range(SSD_N_GROUPS):
        def store_x(r, y, g=g):
            xs_ref[0, g, r:r + CONV_ROWS, :] = y
        conv_silu(_dot(hn_ref[...], wx_ref[:, g * W:(g + 1) * W]), g * W, store_x)
    for ref, w_ref, base in ((bm_ref, wb_ref, SSD_D_INNER), (cm_ref, wc_ref, SSD_D_INNER + SSD_BC_WIDTH)):
        for p in range(SSD_BC_WIDTH // W):
            def store_bc(r, y, ref=ref, p=p):
                ref[0, 2 * p, r:r + CONV_ROWS, :] = y[:, :SSD_D_STATE]
                ref[0, 2 * p + 1, r:r + CONV_ROWS, :] = y[:, SSD_D_STATE:]
            conv_silu(_dot(hn_ref[...], w_ref[:, p * W:(p + 1) * W]), base + p * W, store_bc)


def _proj_xbc(x, nw, wx, wb, wc, cw8, *, tm=512):
    B, S, D = x.shape
    H = CONV_HALO
    G = SSD_N_GROUPS
    nh = S // H
    r = tm // H
    grid = (B, S // tm)
    return pl.pallas_call(
        functools.partial(_proj_xbc_kernel, tm=tm),
        grid=grid,
        in_specs=[pl.BlockSpec((1, H, D), lambda b, i: (b, jnp.maximum(i * r - 1, 0), 0)),
                  pl.BlockSpec((1, tm, D), lambda b, i: (b, i, 0)),
                  pl.BlockSpec((1, H, D), lambda b, i: (b, jnp.minimum((i + 1) * r, nh - 1), 0)),
                  _full(nw.shape), _full(wx.shape), _full(wb.shape), _full(wc.shape),
                  _full(cw8.shape)],
        out_specs=[pl.BlockSpec((1, G, tm, SSD_GROUP_WIDTH), lambda b, i: (b, 0, i, 0)),
                   pl.BlockSpec((1, G, tm, SSD_D_STATE), lambda b, i: (b, 0, i, 0)),
                   pl.BlockSpec((1, G, tm, SSD_D_STATE), lambda b, i: (b, 0, i, 0))],
        out_shape=[jax.ShapeDtypeStruct((B, G, S, SSD_GROUP_WIDTH), BF16),
                   jax.ShapeDtypeStruct((B, G, S, SSD_D_STATE), BF16),
                   jax.ShapeDtypeStruct((B, G, S, SSD_D_STATE), BF16)],
        scratch_shapes=[pltpu.VMEM((tm + 2 * H, D), BF16),
                        pltpu.VMEM((tm + 2 * H, SSD_GROUP_WIDTH), F32)],
        compiler_params=_cparams(2),
        name="proj_xbc",
    )(x, x, x, nw, wx, wb, wc, cw8)


def _proj_qkv_kernel(x_ref, nw_ref, w_ref, cos_ref, sa_ref, sb_ref, *rest, tm):
    outs, (hn_ref, f_ref) = rest[:9], rest[9:]
    hn_ref[...] = _rms_scale(x_ref[0], nw_ref[...]).astype(BF16)
    W = ATT_GROUP_WIDTH
    for typ in range(3):
        for g, d in enumerate(ATT_DILS):
            c = typ * len(ATT_DILS) + g
            a = _dot(hn_ref[...], w_ref[:, c * W:(c + 1) * W])
            if typ < 2:
                parts = []
                for p in range(W // LANES):
                    ch = a[:, p * LANES:(p + 1) * LANES]
                    parts.append(ch * cos_ref[...]
                                 + pltpu.roll(ch, ROPE_DIM // 2, 1) * sa_ref[...]
                                 + pltpu.roll(ch, LANES - ROPE_DIM // 2, 1) * sb_ref[...])
                a = jnp.concatenate(parts, axis=1)
            if typ == 0:
                a = a * (ATT_HEAD_DIM ** -0.5 * LOG2E)
            o_ref = outs[c]
            if d == 1:
                o_ref[0, 0] = a.astype(BF16)
            else:
                for p in range(W // LANES):
                    f_ref[p] = a[:, p * LANES:(p + 1) * LANES]
                for r in range(d):
                    for p in range(W // LANES):
                        o_ref[0, r, :, p * LANES:(p + 1) * LANES] = (
                            f_ref[p, pl.ds(r, tm // d, stride=d), :].astype(BF16))


def _proj_qkv(x, nw, wqkv, cos_t, sa_t, sb_t, *, tm=512):
    B, S, D = x.shape
    W = ATT_GROUP_WIDTH
    grid = (B, S // tm)
    tab = pl.BlockSpec((tm, LANES), lambda b, i: (i, 0))
    out_specs, out_shape = [], []
    for _ in range(3):
        for d in ATT_DILS:
            out_specs.append(pl.BlockSpec((1, d, tm // d, W), lambda b, i: (b, 0, i, 0)))
            out_shape.append(jax.ShapeDtypeStruct((B, d, S // d, W), BF16))
    return pl.pallas_call(
        functools.partial(_proj_qkv_kernel, tm=tm),
        grid=grid,
        in_specs=[pl.BlockSpec((1, tm, D), lambda b, i: (b, i, 0)), _full(nw.shape),
                  _full(wqkv.shape), tab, tab, tab],
        out_specs=out_specs,
        out_shape=out_shape,
        scratch_shapes=[pltpu.VMEM((tm, D), BF16), pltpu.VMEM((W // LANES, tm, LANES), F32)],
        compiler_params=_cparams(2),
        name="proj_qkv",
    )(x, nw, wqkv, cos_t, sa_t, sb_t)


def _ssd_scan_kernel(xs_ref, bm_ref, cm_ref, dt_ref, da_ref, dsk_ref, y_ref,
                     r_ref, nv_ref, qc_ref, st_ref, yacc_ref, *, S):
    L = SSD_CHUNK
    nc = S // L
    J = SSD_HEADS_PER_GROUP
    PR = 8 * J
    NSEG = J * L + J * SSD_HEAD_DIM
    da = da_ref[0]
    dt = dt_ref[0]
    lane_in_chunk = lax.broadcasted_iota(jnp.int32, da.shape, 1) & (L - 1)
    pre, suf = da, da
    k = 1
    while k < L:
        pre = pre + jnp.where(lane_in_chunk >= k, pltpu.roll(pre, k, 1), 0.0)
        suf = suf + jnp.where(lane_in_chunk < L - k, pltpu.roll(suf, S - k, 1), 0.0)
        k *= 2
    tot = pre + suf - da
    is_fwd = lax.broadcasted_iota(jnp.int32, da.shape, 0) < J
    a = jnp.where(is_fwd, pre, suf)
    r_ref[0:2 * J] = dt
    r_ref[2 * J:4 * J] = jnp.exp(tot)
    r_ref[4 * J:6 * J] = jnp.exp(tot - a) * dt
    a_hi = a.astype(BF16).astype(F32)
    a_mid = (a - a_hi).astype(BF16).astype(F32)
    a_lo = (a - a_hi - a_mid).astype(BF16).astype(F32)
    parts = jnp.concatenate([a_hi, a_mid, a_lo, jnp.zeros_like(a)], axis=0)
    nv_ref[...] = (-parts).astype(BF16)
    ones_rows = jnp.ones((PR, L), F32)
    pad_rows = jnp.zeros((LANES - 2 * PR, L), F32)
    for c in range(nc):
        tile = jnp.concatenate([parts[:, c * L:(c + 1) * L], ones_rows, pad_rows], axis=0)
        qc_ref[c * L:(c + 1) * L, :] = tile.T.astype(BF16)
    st_ref[...] = jnp.zeros(st_ref.shape, F32)

    lo = lax.broadcasted_iota(jnp.int32, (L, LANES), 1) < SSD_HEAD_DIM
    lo_row = lo[0:1]
    sel = _head_selectors(L)
    ti = lax.broadcasted_iota(jnp.int32, (L, L), 0)
    si = lax.broadcasted_iota(jnp.int32, (L, L), 1)
    causal = (ti >= si, si >= ti)
    dsk = dsk_ref[0]
    prow = lax.broadcasted_iota(jnp.int32, (PR, NSEG), 0)
    pcol = lax.broadcasted_iota(jnp.int32, (PR, NSEG), 1)
    col_head = jnp.where(pcol < J * L, pcol // L, (pcol - J * L) // SSD_HEAD_DIM)
    live = prow < 6 * J

    def pick_rows(d):
        hit = live & (prow % (2 * J) == col_head + J * d)
        return hit.astype(F32).astype(BF16), (hit & (pcol < J * L)).astype(F32).astype(BF16)

    pick = (pick_rows(0), pick_rows(1))
    zero_rows = jnp.zeros((LANES - 2 * PR, NSEG), BF16)

    def chunk(c, d):
        t0 = pl.multiple_of(c * L, L)
        xc = xs_ref[0, 0, pl.ds(t0, L), :]
        bc = bm_ref[0, 0, pl.ds(t0, L), :]
        cc = cm_ref[0, 0, pl.ds(t0, L), :]
        cb = _dot_nt(cc, bc)
        rr = r_ref[:, pl.ds(t0, L)]
        nv = nv_ref[:, pl.ds(t0, L)]
        nv_wide = jnp.concatenate([nv] * J + [jnp.zeros((PR, NSEG - J * L), BF16)], axis=1)
        rhs_seg = jnp.concatenate([pick[d][0], nv_wide * pick[d][1], zero_rows], axis=0)
        seg_all = _dot(qc_ref[pl.ds(t0, L), :], rhs_seg)
        bt = bc.T
        ms, bw = [], []
        for j in range(J):
            r = J * d + j
            dec = jnp.exp(jnp.where(causal[d], seg_all[:, j * L:(j + 1) * L], -jnp.inf))
            ms.append((cb * dec * rr[r:r + 1, :]).astype(BF16))
            bw.append(bt * rr[4 * J + r:4 * J + r + 1, :].astype(BF16))
        yds, dss = [], []
        for p in range(J // 2):
            xp = xc[:, p * LANES:(p + 1) * LANES]
            rhs = jnp.concatenate([xp * sel[0], xp * sel[1]], axis=0)
            yds.append(_dot(jnp.concatenate([ms[2 * p], ms[2 * p + 1]], axis=1), rhs))
            dss.append(_dot(jnp.concatenate([bw[2 * p], bw[2 * p + 1]], axis=1), rhs))
        yd = jnp.concatenate(yds, axis=1)
        ds = jnp.concatenate(dss, axis=1)

        st = st_ref[d]
        yo = _dot(cc, st.astype(BF16)) * jnp.exp(seg_all[:, J * L:])
        et = rr[2 * J + J * d:2 * J + J * d + J, :]
        drow = jnp.concatenate([jnp.where(lo_row, et[2 * p:2 * p + 1], et[2 * p + 1:2 * p + 2])
                                for p in range(J // 2)], axis=1)
        st_ref[d] = st * drow + ds
        return t0, yd + yo, xc.astype(F32)

    U = SSD_UNROLL

    def steps(i):
        return [(c, d) for u in range(U) for c, d in ((i * U + u, 0), (nc - 1 - i * U - u, 1))]

    def first_touch(i, carry):
        for c, d in steps(i):
            t0, y, _ = chunk(c, d)
            yacc_ref[pl.ds(t0, L), :] = y
        return carry

    def second_touch(i, carry):
        for c, d in steps(i):
            t0, y, xf = chunk(c, d)
            y_ref[0, pl.ds(t0, L), :] = (yacc_ref[pl.ds(t0, L), :] + y + xf * dsk).astype(BF16)
        return carry

    lax.fori_loop(0, nc // 2 // U, first_touch, 0)
    lax.fori_loop(nc // 2 // U, nc // U, second_touch, 0)


def _ssd_scan(xs, bm, cm, dt_t, da_t, dsk):
    B, G, S, W = xs.shape
    N = SSD_D_STATE
    J2 = 2 * SSD_HEADS_PER_GROUP
    assert (S // SSD_CHUNK) % (2 * SSD_UNROLL) == 0
    grp = lambda w: pl.BlockSpec((1, 1, S, w), lambda b, g: (b, g, 0, 0))
    return pl.pallas_call(
        functools.partial(_ssd_scan_kernel, S=S),
        grid=(B, G),
        in_specs=[grp(W), grp(N), grp(N),
                  pl.BlockSpec((1, J2, S), lambda b, g: (b, g, 0)),
                  pl.BlockSpec((1, J2, S), lambda b, g: (b, g, 0)),
                  pl.BlockSpec((1, 1, W), lambda b, g: (g, 0, 0))],
        out_specs=pl.BlockSpec((1, S, W), lambda b, g: (b, 0, g)),
        out_shape=jax.ShapeDtypeStruct((B, S, G * W), BF16),
        scratch_shapes=[pltpu.VMEM((3 * J2, S), F32), pltpu.VMEM((4 * J2, S), BF16),
                        pltpu.VMEM((S, LANES), BF16), pltpu.VMEM((2, N, W), F32),
                        pltpu.VMEM((S, W), F32)],
        compiler_params=_cparams(2),
        name="ssd_scan",
    )(xs, bm, cm, dt_t, da_t, dsk)


def _dil_attn_kernel(*refs, S):
    qkv_refs, bias_ref, o_ref, vm_ref, scr = refs[:9], refs[9], refs[10], refs[11], refs[12:]
    BQ, KW, HD = ATT_BQ, ATT_KW, ATT_HEAD_DIM
    lo_q = lax.broadcasted_iota(jnp.int32, (BQ, LANES), 1) < HD
    sel_q = _head_selectors(BQ)
    eye_q = (lax.broadcasted_iota(jnp.int32, (BQ, BQ), 0)
             == lax.broadcasted_iota(jnp.int32, (BQ, BQ), 1)).astype(F32).astype(BF16)
    sel_k = _head_selectors(KW)

    def split_values(v_ref, n):
        def body(c, carry):
            row = pl.multiple_of(c * KW, KW)
            v = v_ref[0, row // n, pl.ds(row % n, KW), :]
            for h in range(2):
                vm_ref[h, pl.ds(row, KW), :] = v * sel_k[h]
            return carry
        lax.fori_loop(0, S // KW, body, 0)

    def attend(qb, kwin, va, vb, bias_t):
        k_aug = jnp.concatenate([kwin, bias_t], axis=1)
        q_aug = jnp.concatenate(
            [jnp.concatenate([qb * sel_q[h], eye_q], axis=1) for h in range(2)], axis=0)
        s = _dot_nt(q_aug, k_aug)
        m = jnp.max(s, axis=-1, keepdims=True)
        p = jnp.exp2(s - m).astype(BF16)
        v_aug = jnp.concatenate([jnp.concatenate([va, sel_k[0]], axis=1),
                                 jnp.concatenate([vb, sel_k[1]], axis=1)], axis=0)
        accl = _dot(jnp.concatenate([p[:BQ], p[BQ:]], axis=1), v_aug)
        return accl[:, :LANES], jnp.where(lo_q, m[:BQ], m[BQ:]), accl[:, LANES:]

    def block(q_ref, k_ref, n, it):
        nblk = n // BQ
        r = it // nblk
        qs = pl.multiple_of((it % nblk) * BQ, BQ)
        ks = pl.multiple_of(jnp.clip(qs - ATT_HALF, 0, n - KW), ATT_HALF)
        variant = (qs - ks) // ATT_HALF
        vrow = pl.multiple_of(r * n + ks, ATT_HALF)
        parts = attend(q_ref[0, r, pl.ds(qs, BQ), :], k_ref[0, r, pl.ds(ks, KW), :],
                       vm_ref[0, pl.ds(vrow, KW), :], vm_ref[1, pl.ds(vrow, KW), :],
                       bias_ref[variant])
        return r, qs, parts

    for g in range(1, len(ATT_DILS)):
        d = ATT_DILS[g]
        q_ref, k_ref, v_ref = qkv_refs[g], qkv_refs[3 + g], qkv_refs[6 + g]
        g_scr = scr[3 * (g - 1):3 * g]
        split_values(v_ref, S // d)

        def dil_body(i, carry, d=d, q_ref=q_ref, k_ref=k_ref, g_scr=g_scr):
            for u in range(ATT_UNROLL):
                r, qs, parts = block(q_ref, k_ref, S // d, i * ATT_UNROLL + u)
                for ref, val in zip(g_scr, parts):
                    ref[pl.ds(r + d * qs, BQ, stride=d), :] = val
            return carry

        lax.fori_loop(0, S // BQ // ATT_UNROLL, dil_body, 0)

    split_values(qkv_refs[6], S)

    def merge_body(i, carry):
        for u in range(ATT_UNROLL):
            _, qs, parts = block(qkv_refs[0], qkv_refs[3], S, i * ATT_UNROLL + u)
            groups = [parts] + [tuple(ref[pl.ds(qs, BQ), :] for ref in scr[3 * (g - 1):3 * g])
                                for g in range(1, len(ATT_DILS))]
            m = functools.reduce(jnp.maximum, [gm for _, gm, _ in groups])
            ws = [jnp.exp2(gm - m) for _, gm, _ in groups]
            num = functools.reduce(lambda a, b: a + b, [w * ga for w, (ga, _, _) in zip(ws, groups)])
            den = functools.reduce(lambda a, b: a + b, [w * gd for w, (_, _, gd) in zip(ws, groups)])
            o_ref[0, pl.ds(qs, BQ), :] = (num / den).astype(BF16)
        return carry

    lax.fori_loop(0, S // BQ // ATT_UNROLL, merge_body, 0)


def _dil_attn(qkv, bias):
    B = qkv[0].shape[0]
    S = qkv[0].shape[2]
    W = ATT_GROUP_WIDTH
    in_specs = []
    for t in range(3):
        for d in ATT_DILS:
            in_specs.append(pl.BlockSpec((1, d, S // d, LANES), lambda b, p: (b, 0, 0, p)))
    in_specs.append(_full(bias.shape))
    n_scr = 3 * (len(ATT_DILS) - 1)
    return pl.pallas_call(
        functools.partial(_dil_attn_kernel, S=S),
        grid=(B, W // LANES),
        in_specs=in_specs,
        out_specs=pl.BlockSpec((1, S, LANES), lambda b, p: (b, 0, p)),
        out_shape=jax.ShapeDtypeStruct((B, S, W), BF16),
        scratch_shapes=[pltpu.VMEM((2, S, LANES), BF16)] + [pltpu.VMEM((S, LANES), F32)] * n_scr,
        compiler_params=_cparams(2),
        name="dil_attn",
    )(*qkv, bias)


def _mix_merge_kernel(x_ref, ys_ref, zs_ref, ya_ref, gs_ref, nws_ref, ws_ref, wa_ref, wo_ref,
                      nwo_ref, o_ref):
    u = ys_ref[...].astype(F32) * zs_ref[...].astype(F32)
    un = _rms_scale(u, nws_ref[...]).astype(BF16)
    a = _dot(un, ws_ref[...])
    b = _dot(ya_ref[...], wa_ref[...])
    gs = gs_ref[...].astype(F32)
    merged = gs[:, :D_MODEL] * a + gs[:, D_MODEL:] * b
    mix = _dot(merged.astype(BF16), wo_ref[...])
    o_ref[...] = x_ref[...] + _rms_scale(mix, nwo_ref[...])


def _mix_merge(x2, ys, zs, ya, gs, nws, ws, wa, wo, nwo, *, tm=512):
    T, D = x2.shape
    tok = lambda w: pl.BlockSpec((tm, w), lambda i: (i, 0))
    return pl.pallas_call(
        _mix_merge_kernel,
        grid=(T // tm,),
        in_specs=[tok(D), tok(ys.shape[1]), tok(zs.shape[1]), tok(ya.shape[1]), tok(gs.shape[1]),
                  _full(nws.shape), _full(ws.shape), _full(wa.shape), _full(wo.shape),
                  _full(nwo.shape)],
        out_specs=tok(D),
        out_shape=jax.ShapeDtypeStruct((T, D), F32),
        compiler_params=_cparams(1),
        name="mix_merge",
    )(x2, ys, zs, ya, gs, nws, ws, wa, wo, nwo)


def _ffn_kernel(x_ref, nwi_ref, wg_ref, wu_ref, wd_ref, nwo_ref, o_ref, hn_ref, act_ref, *, th):
    x = x_ref[...]
    hn_ref[...] = _rms_scale(x, nwi_ref[...]).astype(BF16)
    for c in range(wg_ref.shape[1] // th):
        sl = slice(c * th, (c + 1) * th)
        gt = _dot(hn_ref[...], wg_ref[:, sl])
        up = _dot(hn_ref[...], wu_ref[:, sl])
        act_ref[:, sl] = (gt * _sigmoid(gt) * up).astype(BF16)
    y = _dot(act_ref[...], wd_ref[...])
    o_ref[...] = x + _rms_scale(y, nwo_ref[...])


def _ffn(x2, nwi, wg, wu, wd, nwo, *, tm=512, th=256):
    T, D = x2.shape
    Hf = wg.shape[1]
    tok = pl.BlockSpec((tm, D), lambda i: (i, 0))
    return pl.pallas_call(
        functools.partial(_ffn_kernel, th=th),
        grid=(T // tm,),
        in_specs=[tok, _full(nwi.shape), _full(wg.shape), _full(wu.shape), _full(wd.shape),
                  _full(nwo.shape)],
        out_specs=tok,
        out_shape=jax.ShapeDtypeStruct((T, D), F32),
        scratch_shapes=[pltpu.VMEM((tm, D), BF16), pltpu.VMEM((tm, Hf), BF16)],
        compiler_params=_cparams(1),
        name="ffn",
    )(x2, nwi, wg, wu, wd, nwo)


def _rope_tables(S):
    pos = jnp.arange(S, dtype=F32)
    inv_freq = ROPE_THETA ** (-jnp.arange(0, ROPE_DIM, 2, dtype=F32) / ROPE_DIM)
    ang = pos[:, None] * inv_freq[None, :]
    h = ROPE_DIM // 2
    cos, sin = jnp.cos(ang), jnp.sin(ang)
    pad = jnp.zeros((S, ATT_HEAD_DIM - ROPE_DIM), F32)
    zh = jnp.zeros((S, h), F32)
    cos_h = jnp.concatenate([cos, cos, pad + 1.0], axis=1)
    sa_h = jnp.concatenate([zh, sin, pad], axis=1)
    sb_h = jnp.concatenate([-sin, zh, pad], axis=1)
    rep = LANES // ATT_HEAD_DIM
    return tuple(jnp.tile(t, (1, rep)) for t in (cos_h, sa_h, sb_h))


def _attn_bias():
    i = np.arange(ATT_BQ)[None, :]
    j = np.arange(ATT_KW)[:, None]
    v = np.arange(3)[:, None, None]
    ok = np.abs(j - i - ATT_HALF * v) <= ATT_HALF
    return jnp.asarray(np.where(ok, 0.0, NEG_BIG), dtype=BF16)


def _layer(x, nmp, w_in, conv_w, conv_b, dt_bias, a_log, d_skip, ssd_nw, w_ssd, w_att, w_out,
           nmpost, nfpre, w_fin, w_fdown, nfpost):
    B, S, D = x.shape
    G, J = SSD_N_GROUPS, SSD_HEADS_PER_GROUP
    row = lambda v: v.reshape(1, -1).astype(F32)
    c0 = SSD_D_INNER
    c1 = c0 + SSD_D_INNER + 2 * SSD_BC_WIDTH
    c2 = c1 + 2 * SSD_N_HEADS
    c3 = c2 + 3 * ATT_WIDTH
    wz = w_in[:, :c0].astype(BF16)
    wx = w_in[:, c0:c0 + SSD_D_INNER].astype(BF16)
    wb = w_in[:, c0 + SSD_D_INNER:c0 + SSD_D_INNER + SSD_BC_WIDTH].astype(BF16)
    wc = w_in[:, c0 + SSD_D_INNER + SSD_BC_WIDTH:c1].astype(BF16)
    perm = lambda v: v.reshape(v.shape[:-1] + (2, G, J)).swapaxes(-3, -2).reshape(v.shape[:-1] + (2 * G * J,))
    wdt = perm(w_in[:, c1:c2])
    wdt = jnp.concatenate([wdt, jnp.zeros((D, LANES - wdt.shape[1]), wdt.dtype)], axis=1).astype(BF16)
    wqkv = w_in[:, c2:c3].astype(BF16)
    wgt = w_in[:, c3:].astype(BF16)
    bias_col = perm(dt_bias.reshape(-1)).reshape(-1, 1).astype(F32)
    alog_col = perm(a_log.reshape(-1)).reshape(-1, 1).astype(F32)
    nw = row(nmp)

    zs, gs, dt_t, da_t = _proj_gate(x, nw, wz, wgt, wdt, bias_col, alog_col)
    cw8 = jnp.concatenate([conv_w, conv_b[None, :],
                           jnp.zeros((8 - SSD_CONV_WIDTH - 1, conv_w.shape[1]), F32)], axis=0)
    xs, bm, cm = _proj_xbc(x, nw, wx, wb, wc, cw8)
    qkv = _proj_qkv(x, nw, wqkv, *_rope_tables(S))

    dsk = jnp.repeat(d_skip.astype(F32), SSD_HEAD_DIM).reshape(G, 1, SSD_GROUP_WIDTH)
    y_ssd = _ssd_scan(xs, bm, cm, dt_t, da_t, dsk)
    y_att = _dil_attn(qkv, _attn_bias())

    T = B * S
    x1 = _mix_merge(x.reshape(T, D), y_ssd.reshape(T, -1), zs.reshape(T, -1), y_att.reshape(T, -1),
                    gs.reshape(T, -1), row(ssd_nw), w_ssd.astype(BF16), w_att.astype(BF16),
                    w_out.astype(BF16), row(nmpost))
    x2 = _ffn(x1, row(nfpre), w_fin[:, :FFN_HIDDEN].astype(BF16), w_fin[:, FFN_HIDDEN:].astype(BF16),
              w_fdown.astype(BF16), row(nfpost))
    return x2.reshape(B, S, D)


def kernel(x, norm_mix_pre, w_in, ssd_conv_w, ssd_conv_b, ssd_dt_bias, ssd_A_log, ssd_D, ssd_norm_w,
           w_ssd_branch, w_attn_branch, w_out, norm_mix_post, norm_ffn_pre, w_ffn_in, w_ffn_down,
           norm_ffn_post):
    for l in range(w_in.shape[0]):
        x = _layer(x, norm_mix_pre[l], w_in[l], ssd_conv_w[l], ssd_conv_b[l], ssd_dt_bias[l],
                   ssd_A_log[l], ssd_D[l], ssd_norm_w[l], w_ssd_branch[l], w_attn_branch[l], w_out[l],
                   norm_mix_post[l], norm_ffn_pre[l], w_ffn_in[l], w_ffn_down[l], norm_ffn_post[l])
    return x
```

```python
import functools

import jax
import jax.numpy as jnp
import numpy as np
from jax import lax
from jax.experimental import pallas as pl
from jax.experimental.pallas import tpu as pltpu

F32 = jnp.float32
BF16 = jnp.bfloat16

NORM_EPS = 1e-6
D_MODEL = 1024
SSD_D_INNER = 2048
SSD_HEAD_DIM = 64
SSD_N_GROUPS = 8
SSD_HEADS_PER_GROUP = 4
SSD_N_HEADS = SSD_N_GROUPS * SSD_HEADS_PER_GROUP
SSD_D_STATE = 128
SSD_GROUP_WIDTH = SSD_HEADS_PER_GROUP * SSD_HEAD_DIM
SSD_BC_WIDTH = SSD_N_GROUPS * SSD_D_STATE
SSD_CONV_WIDTH = 5
SSD_CHUNK = 128
SSD_UNROLL = 4
ATT_HEAD_DIM = 64
ATT_HEADS_PER_GROUP = 8
ATT_PATTERNS = ((128, 1), (512, 4), (2048, 16))
ATT_DILS = tuple(d for _, d in ATT_PATTERNS)
ATT_HALF = 64
assert all(w // (2 * d) == ATT_HALF for w, d in ATT_PATTERNS)
ATT_GROUP_WIDTH = ATT_HEADS_PER_GROUP * ATT_HEAD_DIM
ATT_WIDTH = len(ATT_PATTERNS) * ATT_GROUP_WIDTH
ROPE_THETA = 500000.0
ROPE_DIM = ATT_HEAD_DIM // 4
NEG_BIG = -1e30
LOG2E = 1.4426950408889634
FFN_HIDDEN = 2816
LANES = 128
SUBLANES = 8
CONV_HALO = 16
CONV_ROWS = 128
ATT_BQ = 128
ATT_KW = ATT_BQ + 2 * ATT_HALF
ATT_UNROLL = 8
VMEM_LIMIT = 56 * 1024 * 1024


def _cparams(n_axes):
    return pltpu.CompilerParams(
        dimension_semantics=("parallel",) * n_axes, vmem_limit_bytes=VMEM_LIMIT)


def _rms_scale(x, w):
    ms = jnp.mean(x * x, axis=-1, keepdims=True)
    return x * lax.rsqrt(ms + NORM_EPS) * w


def _sigmoid(a):
    return 1.0 / (1.0 + jnp.exp(-a))


def _softplus(a):
    return jnp.maximum(a, 0.0) + jnp.log(1.0 + jnp.exp(-jnp.abs(a)))


def _dot(a, b):
    return jnp.dot(a, b, preferred_element_type=F32)


def _dot_nt(a, b):
    return lax.dot_general(a, b, (((1,), (1,)), ((), ())), preferred_element_type=F32)


def _dot_tn(a, b):
    return lax.dot_general(a, b, (((0,), (0,)), ((), ())), preferred_element_type=F32)


def _full(shape):
    n = len(shape)
    return pl.BlockSpec(shape, lambda *_: (0,) * n, pipeline_mode=pl.Buffered(1))


def _head_selectors(rows):
    lo = (lax.broadcasted_iota(jnp.int32, (rows, LANES), 1) < LANES // 2).astype(F32)
    return lo.astype(BF16), (1.0 - lo).astype(BF16)


def _proj_gate_kernel(x_ref, nw_ref, wz_ref, wg_ref, wdt_ref, bias_ref, alog_ref,
                      zs_ref, gs_ref, dt_ref, da_ref, hn_ref, *, tn):
    hn_ref[...] = _rms_scale(x_ref[0], nw_ref[...]).astype(BF16)
    for c in range(wz_ref.shape[1] // tn):
        sl = slice(c * tn, (c + 1) * tn)
        a = _dot(hn_ref[...], wz_ref[:, sl])
        zs_ref[0, :, sl] = (a * _sigmoid(a)).astype(BF16)
    for c in range(wg_ref.shape[1] // tn):
        sl = slice(c * tn, (c + 1) * tn)
        a = _dot(hn_ref[...], wg_ref[:, sl])
        gs_ref[0, :, sl] = _sigmoid(a).astype(BF16)
    d = _dot(hn_ref[...], wdt_ref[...])
    n_dt = dt_ref.shape[1]
    dt = _softplus(d.T[:n_dt, :] + bias_ref[...])
    dt_ref[0] = dt
    da_ref[0] = dt * (-jnp.exp(alog_ref[...]))


def _proj_gate(x, nw, wz, wg, wdt, bias_col, alog_col, *, tm=512, tn=256):
    B, S, D = x.shape
    n_dt = bias_col.shape[0]
    grid = (B, S // tm)
    tok = lambda w: pl.BlockSpec((1, tm, w), lambda b, i: (b, i, 0))
    return pl.pallas_call(
        functools.partial(_proj_gate_kernel, tn=tn),
        grid=grid,
        in_specs=[tok(D), _full(nw.shape), _full(wz.shape), _full(wg.shape), _full(wdt.shape),
                  _full(bias_col.shape), _full(alog_col.shape)],
        out_specs=[tok(wz.shape[1]), tok(wg.shape[1]),
                   pl.BlockSpec((1, n_dt, tm), lambda b, i: (b, 0, i)),
                   pl.BlockSpec((1, n_dt, tm), lambda b, i: (b, 0, i))],
        out_shape=[jax.ShapeDtypeStruct((B, S, wz.shape[1]), BF16),
                   jax.ShapeDtypeStruct((B, S, wg.shape[1]), BF16),
                   jax.ShapeDtypeStruct((B, n_dt, S), F32),
                   jax.ShapeDtypeStruct((B, n_dt, S), F32)],
        scratch_shapes=[pltpu.VMEM((tm, D), BF16)],
        compiler_params=_cparams(2),
        name="proj_gate",
    )(x, nw, wz, wg, wdt, bias_col, alog_col)


def _proj_xbc_kernel(xp_ref, x_ref, xn_ref, nw_ref, wx_ref, wb_ref, wc_ref, cw_ref,
                     xs_ref, bm_ref, cm_ref, hn_ref, f_ref, *, tm):
    it = pl.program_id(1)
    H = CONV_HALO
    nw = nw_ref[...]
    keep_p = jnp.where(it > 0, 1.0, 0.0)
    keep_n = jnp.where(it < pl.num_programs(1) - 1, 1.0, 0.0)
    hn_ref[0:H] = (_rms_scale(xp_ref[0], nw) * keep_p).astype(BF16)
    hn_ref[H:H + tm] = _rms_scale(x_ref[0], nw).astype(BF16)
    hn_ref[H + tm:] = (_rms_scale(xn_ref[0], nw) * keep_n).astype(BF16)
    W = SSD_GROUP_WIDTH
    half = SSD_CONV_WIDTH // 2

    sub = lax.broadcasted_iota(jnp.int32, (1, SUBLANES, W), 1)

    def shifted(x3, s):
        xr = pltpu.roll(x3, s % SUBLANES, 1)
        if s > 0:
            return jnp.where(sub < s, jnp.concatenate([xr[-1:], xr[:-1]], axis=0), xr)
        return jnp.where(sub >= SUBLANES + s, jnp.concatenate([xr[1:], xr[:1]], axis=0), xr)

    def conv_silu(acc_ext, col0, store):
        f_ref[...] = acc_ext
        for rb in range(tm // CONV_ROWS):
            r0 = H + rb * CONV_ROWS
            blk = f_ref[r0 - SUBLANES:r0 + CONV_ROWS + SUBLANES, :]
            x3 = blk.reshape(blk.shape[0] // SUBLANES, SUBLANES, W)
            y = cw_ref[SSD_CONV_WIDTH:SSD_CONV_WIDTH + 1, col0:col0 + W]
            for k in range(SSD_CONV_WIDTH):
                sh = x3 if k == half else shifted(x3, half - k)
                sh = sh.reshape(blk.shape)[SUBLANES:SUBLANES + CONV_ROWS, :]
                y = y + cw_ref[k:k + 1, col0:col0 + W] * sh
            store(rb * CONV_ROWS, (y * _sigmoid(y)).astype(BF16))

    for g in range(SSD_N_GROUPS):
        def store_x(r, y, g=g):
            xs_ref[0, g, r:r + CONV_ROWS, :] = y
        conv_silu(_dot(hn_ref[...], wx_ref[:, g * W:(g + 1) * W]), g * W, store_x)
    for ref, w_ref, base in ((bm_ref, wb_ref, SSD_D_INNER), (cm_ref, wc_ref, SSD_D_INNER + SSD_BC_WIDTH)):
        for p in range(SSD_BC_WIDTH // W):
            def store_bc(r, y, ref=ref, p=p):
                ref[0, 2 * p, r:r + CONV_ROWS, :] = y[:, :SSD_D_STATE]
                ref[0, 2 * p + 1, r:r + CONV_ROWS, :] = y[:, SSD_D_STATE:]
            conv_silu(_dot(hn_ref[...], w_ref[:, p * W:(p + 1) * W]), base + p * W, store_bc)


def _proj_xbc(x, nw, wx, wb, wc, cw8, *, tm=512):
    B, S, D = x.shape
    H = CONV_HALO
    G = SSD_N_GROUPS
    nh = S // H
    r = tm // H
    grid = (B, S // tm)
    return pl.pallas_call(
        functools.partial(_proj_xbc_kernel, tm=tm),
        grid=grid,
        in_specs=[pl.BlockSpec((1, H, D), lambda b, i: (b, jnp.maximum(i * r - 1, 0), 0)),
                  pl.BlockSpec((1, tm, D), lambda b, i: (b, i, 0)),
                  pl.BlockSpec((1, H, D), lambda b, i: (b, jnp.minimum((i + 1) * r, nh - 1), 0)),
                  _full(nw.shape), _full(wx.shape), _full(wb.shape), _full(wc.shape),
                  _full(cw8.shape)],
        out_specs=[pl.BlockSpec((1, G, tm, SSD_GROUP_WIDTH), lambda b, i: (b, 0, i, 0)),
                   pl.BlockSpec((1, G, tm, SSD_D_STATE), lambda b, i: (b, 0, i, 0)),
                   pl.BlockSpec((1, G, tm, SSD_D_STATE), lambda b, i: (b, 0, i, 0))],
        out_shape=[jax.ShapeDtypeStruct((B, G, S, SSD_GROUP_WIDTH), BF16),
                   jax.ShapeDtypeStruct((B, G, S, SSD_D_STATE), BF16),
                   jax.ShapeDtypeStruct((B, G, S, SSD_D_STATE), BF16)],
        scratch_shapes=[pltpu.VMEM((tm + 2 * H, D), BF16),
                        pltpu.VMEM((tm + 2 * H, SSD_GROUP_WIDTH), F32)],
        compiler_params=_cparams(2),
        name="proj_xbc",
    )(x, x, x, nw, wx, wb, wc, cw8)


def _proj_qkv_kernel(x_ref, nw_ref, w_ref, cos_ref, sa_ref, sb_ref, *rest, tm):
    outs, (hn_ref, f_ref) = rest[:9], rest[9:]
    hn_ref[...] = _rms_scale(x_ref[0], nw_ref[...]).astype(BF16)
    W = ATT_GROUP_WIDTH
    for typ in range(3):
        for g, d in enumerate(ATT_DILS):
            c = typ * len(ATT_DILS) + g
            a = _dot(hn_ref[...], w_ref[:, c * W:(c + 1) * W])
            if typ < 2:
                parts = []
                for p in range(W // LANES):
                    ch = a[:, p * LANES:(p + 1) * LANES]
                    parts.append(ch * cos_ref[...]
                                 + pltpu.roll(ch, ROPE_DIM // 2, 1) * sa_ref[...]
                                 + pltpu.roll(ch, LANES - ROPE_DIM // 2, 1) * sb_ref[...])
                a = jnp.concatenate(parts, axis=1)
            if typ == 0:
                a = a * (ATT_HEAD_DIM ** -0.5 * LOG2E)
            o_ref = outs[c]
            if d == 1:
                o_ref[0, 0] = a.astype(BF16)
            else:
                for p in range(W // LANES):
                    f_ref[p] = a[:, p * LANES:(p + 1) * LANES]
                for r in range(d):
                    for p in range(W // LANES):
                        o_ref[0, r, :, p * LANES:(p + 1) * LANES] = (
                            f_ref[p, pl.ds(r, tm // d, stride=d), :].astype(BF16))


def _proj_qkv(x, nw, wqkv, cos_t, sa_t, sb_t, *, tm=512):
    B, S, D = x.shape
    W = ATT_GROUP_WIDTH
    grid = (B, S // tm)
    tab = pl.BlockSpec((tm, LANES), lambda b, i: (i, 0))
    out_specs, out_shape = [], []
    for _ in range(3):
        for d in ATT_DILS:
            out_specs.append(pl.BlockSpec((1, d, tm // d, W), lambda b, i: (b, 0, i, 0)))
            out_shape.append(jax.ShapeDtypeStruct((B, d, S // d, W), BF16))
    return pl.pallas_call(
        functools.partial(_proj_qkv_kernel, tm=tm),
        grid=grid,
        in_specs=[pl.BlockSpec((1, tm, D), lambda b, i: (b, i, 0)), _full(nw.shape),
                  _full(wqkv.shape), tab, tab, tab],
        out_specs=out_specs,
        out_shape=out_shape,
        scratch_shapes=[pltpu.VMEM((tm, D), BF16), pltpu.VMEM((W // LANES, tm, LANES), F32)],
        compiler_params=_cparams(2),
        name="proj_qkv",
    )(x, nw, wqkv, cos_t, sa_t, sb_t)


def _ssd_scan_kernel(xs_ref, bm_ref, cm_ref, dt_ref, da_ref, dsk_ref, y_ref,
                     r_ref, qc_ref, st_ref, yacc_ref, xm_ref, *, S):
    L = SSD_CHUNK
    nc = S // L
    J = SSD_HEADS_PER_GROUP
    da = da_ref[0]
    dt = dt_ref[0]
    lane_in_chunk = lax.broadcasted_iota(jnp.int32, da.shape, 1) & (L - 1)
    pre, suf = da, da
    k = 1
    while k < L:
        pre = pre + jnp.where(lane_in_chunk >= k, pltpu.roll(pre, k, 1), 0.0)
        suf = suf + jnp.where(lane_in_chunk < L - k, pltpu.roll(suf, S - k, 1), 0.0)
        k *= 2
    tot = pre + suf - da
    is_fwd = lax.broadcasted_iota(jnp.int32, da.shape, 0) < J
    a = jnp.where(is_fwd, pre, suf)
    ea = jnp.exp(a)
    a2 = a * LOG2E
    r_ref[0:2 * J] = a2 - jnp.log2(dt)
    r_ref[2 * J:4 * J] = jnp.exp(tot)
    r_ref[4 * J:6 * J] = jnp.exp(tot - a) * dt
    pad = jnp.zeros((LANES - 4 * J, L), F32)
    for c in range(nc):
        tile = jnp.concatenate([a2[:, c * L:(c + 1) * L], ea[:, c * L:(c + 1) * L], pad], axis=0)
        qc_ref[c * L:(c + 1) * L, :] = tile.T
    st_ref[...] = jnp.zeros(st_ref.shape, F32)

    lo = lax.broadcasted_iota(jnp.int32, (L, LANES), 1) < SSD_HEAD_DIM
    lo_row = lo[0:1]
    head_col = jnp.where(lo, 0, 1)
    sel = _head_selectors(L)

    def split_x(c, carry):
        t0 = pl.multiple_of(c * L, L)
        xc = xs_ref[0, 0, pl.ds(t0, L), :]
        for h in range(2):
            xm_ref[h, pl.ds(t0, L), :] = jnp.concatenate(
                [xc[:, p * LANES:(p + 1) * LANES] * sel[h] for p in range(J // 2)], axis=1)
        return carry

    lax.fori_loop(0, nc, split_x, 0)
    ti = lax.broadcasted_iota(jnp.int32, (L, L), 0)
    si = lax.broadcasted_iota(jnp.int32, (L, L), 1)
    causal = (ti >= si, si >= ti)
    dsk = dsk_ref[0]

    def chunk(c, d):
        t0 = pl.multiple_of(c * L, L)
        xc = xs_ref[0, 0, pl.ds(t0, L), :]
        bc = bm_ref[0, 0, pl.ds(t0, L), :]
        cc = cm_ref[0, 0, pl.ds(t0, L), :]
        cb = _dot_nt(cc, bc)
        qc = qc_ref[pl.ds(t0, L), :]
        rr = r_ref[:, pl.ds(t0, L)]
        bt = bc.T
        ms, bw = [], []
        for j in range(J):
            r = J * d + j
            seg = qc[:, r:r + 1] - rr[r:r + 1, :]
            dec = jnp.exp2(jnp.where(causal[d], seg, -jnp.inf))
            ms.append((cb * dec).astype(BF16))
            bw.append(bt * rr[4 * J + r:4 * J + r + 1, :].astype(BF16))
        yds, dss = [], []
        for p in range(J // 2):
            rhs = jnp.concatenate([xm_ref[h, pl.ds(t0, L), p * LANES:(p + 1) * LANES]
                                   for h in range(2)], axis=0)
            yds.append(_dot(jnp.concatenate([ms[2 * p], ms[2 * p + 1]], axis=1), rhs))
            dss.append(_dot(jnp.concatenate([bw[2 * p], bw[2 * p + 1]], axis=1), rhs))
        yd = jnp.concatenate(yds, axis=1)
        ds = jnp.concatenate(dss, axis=1)

        escale = jnp.concatenate(
            [jnp.take_along_axis(qc, head_col + (2 * J + J * d + 2 * p), axis=1)
             for p in range(J // 2)], axis=1)
        st = st_ref[d]
        yo = _dot(cc, st.astype(BF16)) * escale
        et = rr[2 * J + J * d:2 * J + J * d + J, :]
        drow = jnp.concatenate([jnp.where(lo_row, et[2 * p:2 * p + 1], et[2 * p + 1:2 * p + 2])
                                for p in range(J // 2)], axis=1)
        st_ref[d] = st * drow + ds
        return t0, yd + yo, xc.astype(F32)

    U = SSD_UNROLL

    def steps(i):
        return [(c, d) for u in range(U) for c, d in ((i * U + u, 0), (nc - 1 - i * U - u, 1))]

    def first_touch(i, carry):
        for c, d in steps(i):
            t0, y, _ = chunk(c, d)
            yacc_ref[pl.ds(t0, L), :] = y
        return carry

    def second_touch(i, carry):
        for c, d in steps(i):
            t0, y, xf = chunk(c, d)
            y_ref[0, pl.ds(t0, L), :] = (yacc_ref[pl.ds(t0, L), :] + y + xf * dsk).astype(BF16)
        return carry

    lax.fori_loop(0, nc // 2 // U, first_touch, 0)
    lax.fori_loop(nc // 2 // U, nc // U, second_touch, 0)


def _ssd_scan(xs, bm, cm, dt_t, da_t, dsk):
    B, G, S, W = xs.shape
    N = SSD_D_STATE
    J2 = 2 * SSD_HEADS_PER_GROUP
    assert (S // SSD_CHUNK) % (2 * SSD_UNROLL) == 0
    grp = lambda w: pl.BlockSpec((1, 1, S, w), lambda b, g: (b, g, 0, 0))
    return pl.pallas_call(
        functools.partial(_ssd_scan_kernel, S=S),
        grid=(B, G),
        in_specs=[grp(W), grp(N), grp(N),
                  pl.BlockSpec((1, J2, S), lambda b, g: (b, g, 0)),
                  pl.BlockSpec((1, J2, S), lambda b, g: (b, g, 0)),
                  pl.BlockSpec((1, 1, W), lambda b, g: (g, 0, 0))],
        out_specs=pl.BlockSpec((1, S, W), lambda b, g: (b, 0, g)),
        out_shape=jax.ShapeDtypeStruct((B, S, G * W), BF16),
        scratch_shapes=[pltpu.VMEM((3 * J2, S), F32),
                        pltpu.VMEM((S, LANES), F32), pltpu.VMEM((2, N, W), F32),
                        pltpu.VMEM((S, W), F32), pltpu.VMEM((2, S, W), BF16)],
        compiler_params=_cparams(2),
        name="ssd_scan",
    )(xs, bm, cm, dt_t, da_t, dsk)


def _dil_attn_kernel(*refs, S):
    qkv_refs, bias_ref, o_ref, vm_ref, scr = refs[:9], refs[9], refs[10], refs[11], refs[12:]
    BQ, KW, HD = ATT_BQ, ATT_KW, ATT_HEAD_DIM
    lo_q = lax.broadcasted_iota(jnp.int32, (BQ, LANES), 1) < HD
    sel_q = _head_selectors(BQ)
    eye_q = (lax.broadcasted_iota(jnp.int32, (BQ, BQ), 0)
             == lax.broadcasted_iota(jnp.int32, (BQ, BQ), 1)).astype(F32).astype(BF16)
    sel_k = _head_selectors(KW)

    def split_values(v_ref, n):
        def body(c, carry):
            row = pl.multiple_of(c * KW, KW)
            v = v_ref[0, row // n, pl.ds(row % n, KW), :]
            for h in range(2):
                vm_ref[h, pl.ds(row, KW), :] = v * sel_k[h]
            return carry
        lax.fori_loop(0, S // KW, body, 0)

    def attend(qb, kwin, va, vb, bias_t):
        k_aug = jnp.concatenate([kwin, bias_t], axis=1)
        q_aug = jnp.concatenate(
            [jnp.concatenate([qb * sel_q[h], eye_q], axis=1) for h in range(2)], axis=0)
        s = _dot_nt(q_aug, k_aug)
        m = jnp.max(s, axis=-1, keepdims=True)
        p = jnp.exp2(s - m).astype(BF16)
        v_aug = jnp.concatenate([jnp.concatenate([va, sel_k[0]], axis=1),
                                 jnp.concatenate([vb, sel_k[1]], axis=1)], axis=0)
        accl = _dot(jnp.concatenate([p[:BQ], p[BQ:]], axis=1), v_aug)
        return accl[:, :LANES], jnp.where(lo_q, m[:BQ], m[BQ:]), accl[:, LANES:]

    def block(q_ref, k_ref, n, it):
        nblk = n // BQ
        r = it // nblk
        qs = pl.multiple_of((it % nblk) * BQ, BQ)
        ks = pl.multiple_of(jnp.clip(qs - ATT_HALF, 0, n - KW), ATT_HALF)
        variant = (qs - ks) // ATT_HALF
        vrow = pl.multiple_of(r * n + ks, ATT_HALF)
        parts = attend(q_ref[0, r, pl.ds(qs, BQ), :], k_ref[0, r, pl.ds(ks, KW), :],
                       vm_ref[0, pl.ds(vrow, KW), :], vm_ref[1, pl.ds(vrow, KW), :],
                       bias_ref[variant])
        return r, qs, parts

    for g in range(1, len(ATT_DILS)):
        d = ATT_DILS[g]
        q_ref, k_ref, v_ref = qkv_refs[g], qkv_refs[3 + g], qkv_refs[6 + g]
        g_scr = scr[3 * (g - 1):3 * g]
        split_values(v_ref, S // d)

        def dil_body(i, carry, d=d, q_ref=q_ref, k_ref=k_ref, g_scr=g_scr):
            for u in range(ATT_UNROLL):
                r, qs, parts = block(q_ref, k_ref, S // d, i * ATT_UNROLL + u)
                for ref, val in zip(g_scr, parts):
                    ref[pl.ds(r + d * qs, BQ, stride=d), :] = val
            return carry

        lax.fori_loop(0, S // BQ // ATT_UNROLL, dil_body, 0)

    split_values(qkv_refs[6], S)

    def merge_body(i, carry):
        for u in range(ATT_UNROLL):
            _, qs, parts = block(qkv_refs[0], qkv_refs[3], S, i * ATT_UNROLL + u)
            groups = [parts] + [tuple(ref[pl.ds(qs, BQ), :] for ref in scr[3 * (g - 1):3 * g])
                                for g in range(1, len(ATT_DILS))]
            m = functools.reduce(jnp.maximum, [gm for _, gm, _ in groups])
            ws = [jnp.exp2(gm - m) for _, gm, _ in groups]
            num = functools.reduce(lambda a, b: a + b, [w * ga for w, (ga, _, _) in zip(ws, groups)])
            den = functools.reduce(lambda a, b: a + b, [w * gd for w, (_, _, gd) in zip(ws, groups)])
            o_ref[0, pl.ds(qs, BQ), :] = (num / den).astype(BF16)
        return carry

    lax.fori_loop(0, S // BQ // ATT_UNROLL, merge_body, 0)


def _dil_attn(qkv, bias):
    B = qkv[0].shape[0]
    S = qkv[0].shape[2]
    W = ATT_GROUP_WIDTH
    in_specs = []
    for t in range(3):
        for d in ATT_DILS:
            in_specs.append(pl.BlockSpec((1, d, S // d, LANES), lambda b, p: (b, 0, 0, p)))
    in_specs.append(_full(bias.shape))
    n_scr = 3 * (len(ATT_DILS) - 1)
    return pl.pallas_call(
        functools.partial(_dil_attn_kernel, S=S),
        grid=(B, W // LANES),
        in_specs=in_specs,
        out_specs=pl.BlockSpec((1, S, LANES), lambda b, p: (b, 0, p)),
        out_shape=jax.ShapeDtypeStruct((B, S, W), BF16),
        scratch_shapes=[pltpu.VMEM((2, S, LANES), BF16)] + [pltpu.VMEM((S, LANES), F32)] * n_scr,
        compiler_params=_cparams(2),
        name="dil_attn",
    )(*qkv, bias)


def _mix_merge_kernel(x_ref, ys_ref, zs_ref, ya_ref, gs_ref, nws_ref, ws_ref, wa_ref, wo_ref,
                      nwo_ref, o_ref):
    u = ys_ref[...].astype(F32) * zs_ref[...].astype(F32)
    un = _rms_scale(u, nws_ref[...]).astype(BF16)
    a = _dot(un, ws_ref[...])
    b = _dot(ya_ref[...], wa_ref[...])
    gs = gs_ref[...].astype(F32)
    merged = gs[:, :D_MODEL] * a + gs[:, D_MODEL:] * b
    mix = _dot(merged.astype(BF16), wo_ref[...])
    o_ref[...] = x_ref[...] + _rms_scale(mix, nwo_ref[...])


def _mix_merge(x2, ys, zs, ya, gs, nws, ws, wa, wo, nwo, *, tm=512):
    T, D = x2.shape
    tok = lambda w: pl.BlockSpec((tm, w), lambda i: (i, 0))
    return pl.pallas_call(
        _mix_merge_kernel,
        grid=(T // tm,),
        in_specs=[tok(D), tok(ys.shape[1]), tok(zs.shape[1]), tok(ya.shape[1]), tok(gs.shape[1]),
                  _full(nws.shape), _full(ws.shape), _full(wa.shape), _full(wo.shape),
                  _full(nwo.shape)],
        out_specs=tok(D),
        out_shape=jax.ShapeDtypeStruct((T, D), F32),
        compiler_params=_cparams(1),
        name="mix_merge",
    )(x2, ys, zs, ya, gs, nws, ws, wa, wo, nwo)


def _ffn_kernel(x_ref, nwi_ref, wg_ref, wu_ref, wd_ref, nwo_ref, o_ref, hn_ref, act_ref, *, th):
    x = x_ref[...]
    hn_ref[...] = _rms_scale(x, nwi_ref[...]).astype(BF16)
    for c in range(wg_ref.shape[1] // th):
        sl = slice(c * th, (c + 1) * th)
        gt = _dot(hn_ref[...], wg_ref[:, sl])
        up = _dot(hn_ref[...], wu_ref[:, sl])
        act_ref[:, sl] = (gt * _sigmoid(gt) * up).astype(BF16)
    y = _dot(act_ref[...], wd_ref[...])
    o_ref[...] = x + _rms_scale(y, nwo_ref[...])


def _ffn(x2, nwi, wg, wu, wd, nwo, *, tm=512, th=256):
    T, D = x2.shape
    Hf = wg.shape[1]
    tok = pl.BlockSpec((tm, D), lambda i: (i, 0))
    return pl.pallas_call(
        functools.partial(_ffn_kernel, th=th),
        grid=(T // tm,),
        in_specs=[tok, _full(nwi.shape), _full(wg.shape), _full(wu.shape), _full(wd.shape),
                  _full(nwo.shape)],
        out_specs=tok,
        out_shape=jax.ShapeDtypeStruct((T, D), F32),
        scratch_shapes=[pltpu.VMEM((tm, D), BF16), pltpu.VMEM((tm, Hf), BF16)],
        compiler_params=_cparams(1),
        name="ffn",
    )(x2, nwi, wg, wu, wd, nwo)


def _rope_tables(S):
    pos = jnp.arange(S, dtype=F32)
    inv_freq = ROPE_THETA ** (-jnp.arange(0, ROPE_DIM, 2, dtype=F32) / ROPE_DIM)
    ang = pos[:, None] * inv_freq[None, :]
    h = ROPE_DIM // 2
    cos, sin = jnp.cos(ang), jnp.sin(ang)
    pad = jnp.zeros((S, ATT_HEAD_DIM - ROPE_DIM), F32)
    zh = jnp.zeros((S, h), F32)
    cos_h = jnp.concatenate([cos, cos, pad + 1.0], axis=1)
    sa_h = jnp.concatenate([zh, sin, pad], axis=1)
    sb_h = jnp.concatenate([-sin, zh, pad], axis=1)
    rep = LANES // ATT_HEAD_DIM
    return tuple(jnp.tile(t, (1, rep)) for t in (cos_h, sa_h, sb_h))


def _attn_bias():
    i = np.arange(ATT_BQ)[None, :]
    j = np.arange(ATT_KW)[:, None]
    v = np.arange(3)[:, None, None]
    ok = np.abs(j - i - ATT_HALF * v) <= ATT_HALF
    return jnp.asarray(np.where(ok, 0.0, NEG_BIG), dtype=BF16)


def _layer(x, nmp, w_in, conv_w, conv_b, dt_bias, a_log, d_skip, ssd_nw, w_ssd, w_att, w_out,
           nmpost, nfpre, w_fin, w_fdown, nfpost):
    B, S, D = x.shape
    G, J = SSD_N_GROUPS, SSD_HEADS_PER_GROUP
    row = lambda v: v.reshape(1, -1).astype(F32)
    c0 = SSD_D_INNER
    c1 = c0 + SSD_D_INNER + 2 * SSD_BC_WIDTH
    c2 = c1 + 2 * SSD_N_HEADS
    c3 = c2 + 3 * ATT_WIDTH
    wz = w_in[:, :c0].astype(BF16)
    wx = w_in[:, c0:c0 + SSD_D_INNER].astype(BF16)
    wb = w_in[:, c0 + SSD_D_INNER:c0 + SSD_D_INNER + SSD_BC_WIDTH].astype(BF16)
    wc = w_in[:, c0 + SSD_D_INNER + SSD_BC_WIDTH:c1].astype(BF16)
    perm = lambda v: v.reshape(v.shape[:-1] + (2, G, J)).swapaxes(-3, -2).reshape(v.shape[:-1] + (2 * G * J,))
    wdt = perm(w_in[:, c1:c2])
    wdt = jnp.concatenate([wdt, jnp.zeros((D, LANES - wdt.shape[1]), wdt.dtype)], axis=1).astype(BF16)
    wqkv = w_in[:, c2:c3].astype(BF16)
    wgt = w_in[:, c3:].astype(BF16)
    bias_col = perm(dt_bias.reshape(-1)).reshape(-1, 1).astype(F32)
    alog_col = perm(a_log.reshape(-1)).reshape(-1, 1).astype(F32)
    nw = row(nmp)

    zs, gs, dt_t, da_t = _proj_gate(x, nw, wz, wgt, wdt, bias_col, alog_col)
    cw8 = jnp.concatenate([conv_w, conv_b[None, :],
                           jnp.zeros((8 - SSD_CONV_WIDTH - 1, conv_w.shape[1]), F32)], axis=0)
    xs, bm, cm = _proj_xbc(x, nw, wx, wb, wc, cw8)
    qkv = _proj_qkv(x, nw, wqkv, *_rope_tables(S))

    dsk = jnp.repeat(d_skip.astype(F32), SSD_HEAD_DIM).reshape(G, 1, SSD_GROUP_WIDTH)
    y_ssd = _ssd_scan(xs, bm, cm, dt_t, da_t, dsk)
    y_att = _dil_attn(qkv, _attn_bias())

    T = B * S
    x1 = _mix_merge(x.reshape(T, D), y_ssd.reshape(T, -1), zs.reshape(T, -1), y_att.reshape(T, -1),
                    gs.reshape(T, -1), row(ssd_nw), w_ssd.astype(BF16), w_att.astype(BF16),
                    w_out.astype(BF16), row(nmpost))
    x2 = _ffn(x1, row(nfpre), w_fin[:, :FFN_HIDDEN].astype(BF16), w_fin[:, FFN_HIDDEN:].astype(BF16),
              w_fdown.astype(BF16), row(nfpost))
    return x2.reshape(B, S, D)


def kernel(x, norm_mix_pre, w_in, ssd_conv_w, ssd_conv_b, ssd_dt_bias, ssd_A_log, ssd_D, ssd_norm_w,
           w_ssd_branch, w_attn_branch, w_out, norm_mix_post, norm_ffn_pre, w_ffn_in, w_ffn_down,
           norm_ffn_post):
    for l in range(w_in.shape[0]):
        x = _layer(x, norm_mix_pre[l], w_in[l], ssd_conv_w[l], ssd_conv_b[l], ssd_dt_bias[l],
                   ssd_A_log[l], ssd_D[l], ssd_norm_w[l], w_ssd_branch[l], w_attn_branch[l], w_out[l],
                   norm_mix_post[l], norm_ffn_pre[l], w_ffn_in[l], w_ffn_down[l], norm_ffn_post[l])
    return x
```

```python
import functools

import jax
import jax.numpy as jnp
import numpy as np
from jax import lax
from jax.experimental import pallas as pl
from jax.experimental.pallas import tpu as pltpu

F32 = jnp.float32
BF16 = jnp.bfloat16

NORM_EPS = 1e-6
D_MODEL = 1024
SSD_D_INNER = 2048
SSD_HEAD_DIM = 64
SSD_N_GROUPS = 8
SSD_HEADS_PER_GROUP = 4
SSD_N_HEADS = SSD_N_GROUPS * SSD_HEADS_PER_GROUP
SSD_D_STATE = 128
SSD_GROUP_WIDTH = SSD_HEADS_PER_GROUP * SSD_HEAD_DIM
SSD_BC_WIDTH = SSD_N_GROUPS * SSD_D_STATE
SSD_CONV_WIDTH = 5
SSD_CHUNK = 128
SSD_UNROLL = 4
ATT_HEAD_DIM = 64
ATT_HEADS_PER_GROUP = 8
ATT_PATTERNS = ((128, 1), (512, 4), (2048, 16))
ATT_DILS = tuple(d for _, d in ATT_PATTERNS)
ATT_HALF = 64
assert all(w // (2 * d) == ATT_HALF for w, d in ATT_PATTERNS)
ATT_GROUP_WIDTH = ATT_HEADS_PER_GROUP * ATT_HEAD_DIM
ATT_WIDTH = len(ATT_PATTERNS) * ATT_GROUP_WIDTH
ROPE_THETA = 500000.0
ROPE_DIM = ATT_HEAD_DIM // 4
NEG_BIG = -1e30
LOG2E = 1.4426950408889634
FFN_HIDDEN = 2816
LANES = 128
SUBLANES = 8
CONV_HALO = 16
CONV_ROWS = 256
ATT_BQ = 128
ATT_KW = ATT_BQ + 2 * ATT_HALF
ATT_UNROLL = 8
VMEM_LIMIT = 56 * 1024 * 1024


def _cparams(n_axes):
    return pltpu.CompilerParams(
        dimension_semantics=("parallel",) * n_axes, vmem_limit_bytes=VMEM_LIMIT)


def _rms_scale(x, w):
    ms = jnp.mean(x * x, axis=-1, keepdims=True)
    return x * lax.rsqrt(ms + NORM_EPS) * w


def _sigmoid(a):
    return 1.0 / (1.0 + jnp.exp(-a))


def _softplus(a):
    return jnp.maximum(a, 0.0) + jnp.log(1.0 + jnp.exp(-jnp.abs(a)))


def _dot(a, b):
    return jnp.dot(a, b, preferred_element_type=F32)


def _dot_nt(a, b):
    return lax.dot_general(a, b, (((1,), (1,)), ((), ())), preferred_element_type=F32)


def _dot_tn(a, b):
    return lax.dot_general(a, b, (((0,), (0,)), ((), ())), preferred_element_type=F32)


def _full(shape):
    n = len(shape)
    return pl.BlockSpec(shape, lambda *_: (0,) * n, pipeline_mode=pl.Buffered(1))


def _cols(rows, width, block):
    return pl.BlockSpec((rows, width), lambda *_: (0, block), pipeline_mode=pl.Buffered(1))


def _head_selectors(rows):
    lo = (lax.broadcasted_iota(jnp.int32, (rows, LANES), 1) < LANES // 2).astype(F32)
    return lo.astype(BF16), (1.0 - lo).astype(BF16)


def _proj_gate_kernel(x_ref, nw_ref, wz_ref, wg_ref, wdt_ref, bias_ref, alog_ref,
                      zs_ref, gs_ref, dt_ref, da_ref, hn_ref, *, tn):
    hn_ref[...] = _rms_scale(x_ref[0], nw_ref[...]).astype(BF16)
    for c in range(wz_ref.shape[1] // tn):
        sl = slice(c * tn, (c + 1) * tn)
        a = _dot(hn_ref[...], wz_ref[:, sl])
        zs_ref[0, :, sl] = (a * _sigmoid(a)).astype(BF16)
    for c in range(wg_ref.shape[1] // tn):
        sl = slice(c * tn, (c + 1) * tn)
        a = _dot(hn_ref[...], wg_ref[:, sl])
        gs_ref[0, :, sl] = _sigmoid(a).astype(BF16)
    d = _dot(hn_ref[...], wdt_ref[...])
    n_dt = dt_ref.shape[1]
    dt = _softplus(d.T[:n_dt, :] + bias_ref[...])
    dt_ref[0] = dt
    da_ref[0] = dt * (-jnp.exp(alog_ref[...]))


def _proj_gate(x, nw, w16, wg, wdt, bias_col, alog_col, *, tm=512, tn=256):
    B, S, D = x.shape
    n_dt = bias_col.shape[0]
    nz = SSD_D_INNER
    grid = (B, S // tm)
    tok = lambda w: pl.BlockSpec((1, tm, w), lambda b, i: (b, i, 0))
    return pl.pallas_call(
        functools.partial(_proj_gate_kernel, tn=tn),
        grid=grid,
        in_specs=[tok(D), _full(nw.shape), _cols(D, nz, 0), _full(wg.shape), _full(wdt.shape),
                  _full(bias_col.shape), _full(alog_col.shape)],
        out_specs=[tok(nz), tok(wg.shape[1]),
                   pl.BlockSpec((1, n_dt, tm), lambda b, i: (b, 0, i)),
                   pl.BlockSpec((1, n_dt, tm), lambda b, i: (b, 0, i))],
        out_shape=[jax.ShapeDtypeStruct((B, S, nz), BF16),
                   jax.ShapeDtypeStruct((B, S, wg.shape[1]), BF16),
                   jax.ShapeDtypeStruct((B, n_dt, S), F32),
                   jax.ShapeDtypeStruct((B, n_dt, S), F32)],
        scratch_shapes=[pltpu.VMEM((tm, D), BF16)],
        compiler_params=_cparams(2),
        name="proj_gate",
    )(x, nw, w16, wg, wdt, bias_col, alog_col)


def _proj_xbc_kernel(xp_ref, x_ref, xn_ref, nw_ref, wx_ref, wb_ref, wc_ref, cw_ref,
                     xs_ref, bm_ref, cm_ref, hn_ref, hh_ref, p_ref, f_ref, q_ref, *, tm):
    it = pl.program_id(1)
    H = CONV_HALO
    D = x_ref.shape[2]
    W = SSD_GROUP_WIDTH
    SL = SUBLANES
    BLK = SL * SL
    nb = tm // BLK
    GB = CONV_ROWS // BLK
    nw = nw_ref[...]
    xn = _rms_scale(x_ref[0], nw)
    for p in range(D // LANES):
        p_ref[p] = xn[:, p * LANES:(p + 1) * LANES]
    parts = []
    for i in range(SL):
        rows = jnp.concatenate([p_ref[p, pl.ds(i, tm // SL, stride=SL), :]
                                for p in range(D // LANES)], axis=1)
        parts.append(rows.reshape(nb, SL, D))
    hn_ref[...] = jnp.stack(parts, axis=1).reshape(tm, D).astype(BF16)
    keep_p = jnp.where(it > 0, 1.0, 0.0)
    keep_n = jnp.where(it < pl.num_programs(1) - 1, 1.0, 0.0)
    hh_ref[0:H] = (_rms_scale(xp_ref[0], nw) * keep_p).astype(BF16)
    hh_ref[H:] = (_rms_scale(xn_ref[0], nw) * keep_n).astype(BF16)

    sub = lax.broadcasted_iota(jnp.int32, (1, SL, W), 1)
    up = lambda v: pltpu.roll(v, SL - 1, v.ndim - 2)
    down = lambda v: pltpu.roll(v, 1, v.ndim - 2)

    def conv_silu(w, col0, store):
        f_ref[...] = _dot(hn_ref[...], w)
        halo = _dot(hh_ref[...], w)
        prev_tail, next_head = halo[H - SL:H], halo[H:H + SL]

        def vreg(b, i):
            return f_ref[b * BLK + i * SL:b * BLK + (i + 1) * SL, :]

        for gb in range(nb // GB):
            blocks = range(gb * GB, (gb + 1) * GB)
            a = f_ref[gb * CONV_ROWS:(gb + 1) * CONV_ROWS, :].reshape(GB, SL, SL, W)
            n0 = jnp.stack([vreg(b + 1, 0) if b + 1 < nb else next_head for b in blocks])
            n1 = jnp.stack([vreg(b + 1, 1) if b + 1 < nb else up(next_head) for b in blocks])
            p7 = jnp.stack([vreg(b - 1, SL - 1) if b > 0 else prev_tail for b in blocks])
            p6 = jnp.stack([vreg(b - 1, SL - 2) if b > 0 else down(prev_tail) for b in blocks])
            wp0 = jnp.where(sub < SL - 1, up(a[:, 0]), up(n0))[:, None]
            wp1 = jnp.where(sub < SL - 1, up(a[:, 1]), up(n1))[:, None]
            wm7 = jnp.where(sub >= 1, down(a[:, SL - 1]), down(p7))[:, None]
            wm6 = jnp.where(sub >= 1, down(a[:, SL - 2]), down(p6))[:, None]
            taps = (jnp.concatenate([wm6, wm7, a[:, :SL - 2]], axis=1),
                    jnp.concatenate([wm7, a[:, :SL - 1]], axis=1),
                    a,
                    jnp.concatenate([a[:, 1:], wp0], axis=1),
                    jnp.concatenate([a[:, 2:], wp0, wp1], axis=1))
            y = cw_ref[SSD_CONV_WIDTH:SSD_CONV_WIDTH + 1, col0:col0 + W]
            for k in range(SSD_CONV_WIDTH):
                y = y + cw_ref[k:k + 1, col0:col0 + W] * taps[k]
            y = (y * _sigmoid(y)).reshape(CONV_ROWS, W)
            for p in range(W // LANES):
                q_ref[p] = y[:, p * LANES:(p + 1) * LANES]
            nat = jnp.stack(
                [jnp.concatenate([q_ref[p, pl.ds(j, CONV_ROWS // SL, stride=SL), :]
                                  for p in range(W // LANES)], axis=1).reshape(GB, SL, W)
                 for j in range(SL)], axis=1)
            store(gb * CONV_ROWS, nat.reshape(CONV_ROWS, W).astype(BF16))

    for g in range(SSD_N_GROUPS):
        def store_x(r, y, g=g):
            xs_ref[0, g, r:r + CONV_ROWS, :] = y
        conv_silu(wx_ref[:, g * W:(g + 1) * W], g * W, store_x)
    for ref, w_ref, base in ((bm_ref, wb_ref, SSD_D_INNER), (cm_ref, wc_ref, SSD_D_INNER + SSD_BC_WIDTH)):
        for p in range(SSD_BC_WIDTH // W):
            def store_bc(r, y, ref=ref, p=p):
                ref[0, 2 * p, r:r + CONV_ROWS, :] = y[:, :SSD_D_STATE]
                ref[0, 2 * p + 1, r:r + CONV_ROWS, :] = y[:, SSD_D_STATE:]
            conv_silu(w_ref[:, p * W:(p + 1) * W], base + p * W, store_bc)


def _proj_xbc(x, nw, w16, cw8, *, tm=512):
    B, S, D = x.shape
    H = CONV_HALO
    G = SSD_N_GROUPS
    nh = S // H
    r = tm // H
    grid = (B, S // tm)
    x_blk = 1
    b_blk = 2 * SSD_D_INNER // SSD_BC_WIDTH
    c_blk = b_blk + 1
    return pl.pallas_call(
        functools.partial(_proj_xbc_kernel, tm=tm),
        grid=grid,
        in_specs=[pl.BlockSpec((1, H, D), lambda b, i: (b, jnp.maximum(i * r - 1, 0), 0)),
                  pl.BlockSpec((1, tm, D), lambda b, i: (b, i, 0)),
                  pl.BlockSpec((1, H, D), lambda b, i: (b, jnp.minimum((i + 1) * r, nh - 1), 0)),
                  _full(nw.shape), _cols(D, SSD_D_INNER, x_blk), _cols(D, SSD_BC_WIDTH, b_blk),
                  _cols(D, SSD_BC_WIDTH, c_blk), _full(cw8.shape)],
        out_specs=[pl.BlockSpec((1, G, tm, SSD_GROUP_WIDTH), lambda b, i: (b, 0, i, 0)),
                   pl.BlockSpec((1, G, tm, SSD_D_STATE), lambda b, i: (b, 0, i, 0)),
                   pl.BlockSpec((1, G, tm, SSD_D_STATE), lambda b, i: (b, 0, i, 0))],
        out_shape=[jax.ShapeDtypeStruct((B, G, S, SSD_GROUP_WIDTH), BF16),
                   jax.ShapeDtypeStruct((B, G, S, SSD_D_STATE), BF16),
                   jax.ShapeDtypeStruct((B, G, S, SSD_D_STATE), BF16)],
        scratch_shapes=[pltpu.VMEM((tm, D), BF16), pltpu.VMEM((2 * H, D), BF16),
                        pltpu.VMEM((D // LANES, tm, LANES), F32),
                        pltpu.VMEM((tm, SSD_GROUP_WIDTH), F32),
                        pltpu.VMEM((SSD_GROUP_WIDTH // LANES, CONV_ROWS, LANES), F32)],
        compiler_params=_cparams(2),
        name="proj_xbc",
    )(x, x, x, nw, w16, w16, w16, cw8)


def _proj_qkv_kernel(x_ref, nw_ref, w_ref, cos_ref, sa_ref, sb_ref, *rest, tm):
    outs, (hn_ref, f_ref) = rest[:9], rest[9:]
    hn_ref[...] = _rms_scale(x_ref[0], nw_ref[...]).astype(BF16)
    W = ATT_GROUP_WIDTH
    for typ in range(3):
        for g, d in enumerate(ATT_DILS):
            c = typ * len(ATT_DILS) + g
            a = _dot(hn_ref[...], w_ref[:, c * W:(c + 1) * W])
            if typ < 2:
                parts = []
                for p in range(W // LANES):
                    ch = a[:, p * LANES:(p + 1) * LANES]
                    parts.append(ch * cos_ref[...]
                                 + pltpu.roll(ch, ROPE_DIM // 2, 1) * sa_ref[...]
                                 + pltpu.roll(ch, LANES - ROPE_DIM // 2, 1) * sb_ref[...])
                a = jnp.concatenate(parts, axis=1)
            if typ == 0:
                a = a * (ATT_HEAD_DIM ** -0.5 * LOG2E)
            o_ref = outs[c]
            if d == 1:
                o_ref[0, 0] = a.astype(BF16)
            else:
                for p in range(W // LANES):
                    f_ref[p] = a[:, p * LANES:(p + 1) * LANES]
                for r in range(d):
                    for p in range(W // LANES):
                        o_ref[0, r, :, p * LANES:(p + 1) * LANES] = (
                            f_ref[p, pl.ds(r, tm // d, stride=d), :].astype(BF16))


def _proj_qkv(x, nw, wqkv, cos_t, sa_t, sb_t, *, tm=512):
    B, S, D = x.shape
    W = ATT_GROUP_WIDTH
    grid = (B, S // tm)
    tab = pl.BlockSpec((tm, LANES), lambda b, i: (i, 0))
    out_specs, out_shape = [], []
    for _ in range(3):
        for d in ATT_DILS:
            out_specs.append(pl.BlockSpec((1, d, tm // d, W), lambda b, i: (b, 0, i, 0)))
            out_shape.append(jax.ShapeDtypeStruct((B, d, S // d, W), BF16))
    return pl.pallas_call(
        functools.partial(_proj_qkv_kernel, tm=tm),
        grid=grid,
        in_specs=[pl.BlockSpec((1, tm, D), lambda b, i: (b, i, 0)), _full(nw.shape),
                  _full(wqkv.shape), tab, tab, tab],
        out_specs=out_specs,
        out_shape=out_shape,
        scratch_shapes=[pltpu.VMEM((tm, D), BF16), pltpu.VMEM((W // LANES, tm, LANES), F32)],
        compiler_params=_cparams(2),
        name="proj_qkv",
    )(x, nw, wqkv, cos_t, sa_t, sb_t)


def _ssd_scan_kernel(xs_ref, bm_ref, cm_ref, dt_ref, da_ref, dsk_ref, y_ref,
                     r_ref, qc_ref, st_ref, yacc_ref, xm_ref, *, S):
    L = SSD_CHUNK
    nc = S // L
    J = SSD_HEADS_PER_GROUP
    da = da_ref[0]
    dt = dt_ref[0]
    lane_in_chunk = lax.broadcasted_iota(jnp.int32, da.shape, 1) & (L - 1)
    pre, suf = da, da
    k = 1
    while k < L:
        pre = pre + jnp.where(lane_in_chunk >= k, pltpu.roll(pre, k, 1), 0.0)
        suf = suf + jnp.where(lane_in_chunk < L - k, pltpu.roll(suf, S - k, 1), 0.0)
        k *= 2
    tot = pre + suf - da
    is_fwd = lax.broadcasted_iota(jnp.int32, da.shape, 0) < J
    a = jnp.where(is_fwd, pre, suf)
    ea = jnp.exp(a)
    a2 = a * LOG2E
    r_ref[0:2 * J] = a2 - jnp.log2(dt)
    r_ref[2 * J:4 * J] = jnp.exp(tot)
    r_ref[4 * J:6 * J] = jnp.exp(tot - a) * dt
    pad = jnp.zeros((LANES - 4 * J, L), F32)
    for c in range(nc):
        tile = jnp.concatenate([a2[:, c * L:(c + 1) * L], ea[:, c * L:(c + 1) * L], pad], axis=0)
        qc_ref[c * L:(c + 1) * L, :] = tile.T
    st_ref[...] = jnp.zeros(st_ref.shape, F32)

    lo = lax.broadcasted_iota(jnp.int32, (L, LANES), 1) < SSD_HEAD_DIM
    lo_row = lo[0:1]
    head_col = jnp.where(lo, 0, 1)
    sel = _head_selectors(L)

    def split_x(c, carry):
        t0 = pl.multiple_of(c * L, L)
        xc = xs_ref[0, 0, pl.ds(t0, L), :]
        for h in range(2):
            xm_ref[h, pl.ds(t0, L), :] = jnp.concatenate(
                [xc[:, p * LANES:(p + 1) * LANES] * sel[h] for p in range(J // 2)], axis=1)
        return carry

    lax.fori_loop(0, nc, split_x, 0)
    ti = lax.broadcasted_iota(jnp.int32, (L, L), 0)
    si = lax.broadcasted_iota(jnp.int32, (L, L), 1)
    causal = (ti >= si, si >= ti)
    dsk = dsk_ref[0]

    def chunk(c, d):
        t0 = pl.multiple_of(c * L, L)
        xc = xs_ref[0, 0, pl.ds(t0, L), :]
        bc = bm_ref[0, 0, pl.ds(t0, L), :]
        cc = cm_ref[0, 0, pl.ds(t0, L), :]
        cb = _dot_nt(cc, bc)
        qc = qc_ref[pl.ds(t0, L), :]
        rr = r_ref[:, pl.ds(t0, L)]
        bt = bc.T
        ms, bw = [], []
        for j in range(J):
            r = J * d + j
            seg = qc[:, r:r + 1] - rr[r:r + 1, :]
            dec = jnp.exp2(jnp.where(causal[d], seg, -jnp.inf))
            ms.append((cb * dec).astype(BF16))
            bw.append(bt * rr[4 * J + r:4 * J + r + 1, :].astype(BF16))
        yds, dss = [], []
        for p in range(J // 2):
            rhs = jnp.concatenate([xm_ref[h, pl.ds(t0, L), p * LANES:(p + 1) * LANES]
                                   for h in range(2)], axis=0)
            yds.append(_dot(jnp.concatenate([ms[2 * p], ms[2 * p + 1]], axis=1), rhs))
            dss.append(_dot(jnp.concatenate([bw[2 * p], bw[2 * p + 1]], axis=1), rhs))
        yd = jnp.concatenate(yds, axis=1)
        ds = jnp.concatenate(dss, axis=1)

        escale = jnp.concatenate(
            [jnp.take_along_axis(qc, head_col + (2 * J + J * d + 2 * p), axis=1)
             for p in range(J // 2)], axis=1)
        st = st_ref[d]
        yo = _dot(cc, st.astype(BF16)) * escale
        et = rr[2 * J + J * d:2 * J + J * d + J, :]
        drow = jnp.concatenate([jnp.where(lo_row, et[2 * p:2 * p + 1], et[2 * p + 1:2 * p + 2])
                                for p in range(J // 2)], axis=1)
        st_ref[d] = st * drow + ds
        return t0, yd + yo, xc.astype(F32)

    U = SSD_UNROLL

    def steps(i):
        return [(c, d) for u in range(U) for c, d in ((i * U + u, 0), (nc - 1 - i * U - u, 1))]

    def first_touch(i, carry):
        for c, d in steps(i):
            t0, y, _ = chunk(c, d)
            yacc_ref[pl.ds(t0, L), :] = y
        return carry

    def second_touch(i, carry):
        for c, d in steps(i):
            t0, y, xf = chunk(c, d)
            y_ref[0, pl.ds(t0, L), :] = (yacc_ref[pl.ds(t0, L), :] + y + xf * dsk).astype(BF16)
        return carry

    lax.fori_loop(0, nc // 2 // U, first_touch, 0)
    lax.fori_loop(nc // 2 // U, nc // U, second_touch, 0)


def _ssd_scan(xs, bm, cm, dt_t, da_t, dsk):
    B, G, S, W = xs.shape
    N = SSD_D_STATE
    J2 = 2 * SSD_HEADS_PER_GROUP
    assert (S // SSD_CHUNK) % (2 * SSD_UNROLL) == 0
    grp = lambda w: pl.BlockSpec((1, 1, S, w), lambda b, g: (b, g, 0, 0))
    return pl.pallas_call(
        functools.partial(_ssd_scan_kernel, S=S),
        grid=(B, G),
        in_specs=[grp(W), grp(N), grp(N),
                  pl.BlockSpec((1, J2, S), lambda b, g: (b, g, 0)),
                  pl.BlockSpec((1, J2, S), lambda b, g: (b, g, 0)),
                  pl.BlockSpec((1, 1, W), lambda b, g: (g, 0, 0))],
        out_specs=pl.BlockSpec((1, S, W), lambda b, g: (b, 0, g)),
        out_shape=jax.ShapeDtypeStruct((B, S, G * W), BF16),
        scratch_shapes=[pltpu.VMEM((3 * J2, S), F32),
                        pltpu.VMEM((S, LANES), F32), pltpu.VMEM((2, N, W), F32),
                        pltpu.VMEM((S, W), F32), pltpu.VMEM((2, S, W), BF16)],
        compiler_params=_cparams(2),
        name="ssd_scan",
    )(xs, bm, cm, dt_t, da_t, dsk)


def _dil_attn_kernel(*refs, S):
    qkv_refs, bias_ref, o_ref, vm_ref, scr = refs[:9], refs[9], refs[10], refs[11], refs[12:]
    BQ, KW, HD = ATT_BQ, ATT_KW, ATT_HEAD_DIM
    lo_q = lax.broadcasted_iota(jnp.int32, (BQ, LANES), 1) < HD
    sel_q = _head_selectors(BQ)
    eye_q = (lax.broadcasted_iota(jnp.int32, (BQ, BQ), 0)
             == lax.broadcasted_iota(jnp.int32, (BQ, BQ), 1)).astype(F32).astype(BF16)
    sel_k = _head_selectors(KW)

    def split_values(v_ref, n):
        def body(c, carry):
            row = pl.multiple_of(c * KW, KW)
            v = v_ref[0, row // n, pl.ds(row % n, KW), :]
            for h in range(2):
                vm_ref[h, pl.ds(row, KW), :] = v * sel_k[h]
            return carry
        lax.fori_loop(0, S // KW, body, 0)

    def attend(qb, kwin, va, vb, bias_t):
        k_aug = jnp.concatenate([kwin, bias_t], axis=1)
        q_aug = jnp.concatenate(
            [jnp.concatenate([qb * sel_q[h], eye_q], axis=1) for h in range(2)], axis=0)
        s = _dot_nt(q_aug, k_aug)
        m = jnp.max(s, axis=-1, keepdims=True)
        p = jnp.exp2(s - m).astype(BF16)
        v_aug = jnp.concatenate([jnp.concatenate([va, sel_k[0]], axis=1),
                                 jnp.concatenate([vb, sel_k[1]], axis=1)], axis=0)
        accl = _dot(jnp.concatenate([p[:BQ], p[BQ:]], axis=1), v_aug)
        return accl[:, :LANES], jnp.where(lo_q, m[:BQ], m[BQ:]), accl[:, LANES:]

    def block(q_ref, k_ref, n, it):
        nblk = n // BQ
        r = it // nblk
        qs = pl.multiple_of((it % nblk) * BQ, BQ)
        ks = pl.multiple_of(jnp.clip(qs - ATT_HALF, 0, n - KW), ATT_HALF)
        variant = (qs - ks) // ATT_HALF
        vrow = pl.multiple_of(r * n + ks, ATT_HALF)
        parts = attend(q_ref[0, r, pl.ds(qs, BQ), :], k_ref[0, r, pl.ds(ks, KW), :],
                       vm_ref[0, pl.ds(vrow, KW), :], vm_ref[1, pl.ds(vrow, KW), :],
                       bias_ref[variant])
        return r, qs, parts

    for g in range(1, len(ATT_DILS)):
        d = ATT_DILS[g]
        q_ref, k_ref, v_ref = qkv_refs[g], qkv_refs[3 + g], qkv_refs[6 + g]
        g_scr = scr[3 * (g - 1):3 * g]
        split_values(v_ref, S // d)

        def dil_body(i, carry, d=d, q_ref=q_ref, k_ref=k_ref, g_scr=g_scr):
            for u in range(ATT_UNROLL):
                r, qs, parts = block(q_ref, k_ref, S // d, i * ATT_UNROLL + u)
                for ref, val in zip(g_scr, parts):
                    ref[pl.ds(r + d * qs, BQ, stride=d), :] = val
            return carry

        lax.fori_loop(0, S // BQ // ATT_UNROLL, dil_body, 0)

    split_values(qkv_refs[6], S)

    def merge_body(i, carry):
        for u in range(ATT_UNROLL):
            _, qs, parts = block(qkv_refs[0], qkv_refs[3], S, i * ATT_UNROLL + u)
            groups = [parts] + [tuple(ref[pl.ds(qs, BQ), :] for ref in scr[3 * (g - 1):3 * g])
                                for g in range(1, len(ATT_DILS))]
            m = functools.reduce(jnp.maximum, [gm for _, gm, _ in groups])
            ws = [jnp.exp2(gm - m) for _, gm, _ in groups]
            num = functools.reduce(lambda a, b: a + b, [w * ga for w, (ga, _, _) in zip(ws, groups)])
            den = functools.reduce(lambda a, b: a + b, [w * gd for w, (_, _, gd) in zip(ws, groups)])
            o_ref[0, pl.ds(qs, BQ), :] = (num / den).astype(BF16)
        return carry

    lax.fori_loop(0, S // BQ // ATT_UNROLL, merge_body, 0)


def _dil_attn(qkv, bias):
    B = qkv[0].shape[0]
    S = qkv[0].shape[2]
    W = ATT_GROUP_WIDTH
    in_specs = []
    for t in range(3):
        for d in ATT_DILS:
            in_specs.append(pl.BlockSpec((1, d, S // d, LANES), lambda b, p: (b, 0, 0, p)))
    in_specs.append(_full(bias.shape))
    n_scr = 3 * (len(ATT_DILS) - 1)
    return pl.pallas_call(
        functools.partial(_dil_attn_kernel, S=S),
        grid=(B, W // LANES),
        in_specs=in_specs,
        out_specs=pl.BlockSpec((1, S, LANES), lambda b, p: (b, 0, p)),
        out_shape=jax.ShapeDtypeStruct((B, S, W), BF16),
        scratch_shapes=[pltpu.VMEM((2, S, LANES), BF16)] + [pltpu.VMEM((S, LANES), F32)] * n_scr,
        compiler_params=_cparams(2),
        name="dil_attn",
    )(*qkv, bias)


def _mix_merge_kernel(x_ref, ys_ref, zs_ref, ya_ref, gs_ref, nws_ref, ws_ref, wa_ref, wo_ref,
                      nwo_ref, o_ref):
    u = ys_ref[...].astype(F32) * zs_ref[...].astype(F32)
    un = _rms_scale(u, nws_ref[...]).astype(BF16)
    a = _dot(un, ws_ref[...])
    b = _dot(ya_ref[...], wa_ref[...])
    gs = gs_ref[...].astype(F32)
    merged = gs[:, :D_MODEL] * a + gs[:, D_MODEL:] * b
    mix = _dot(merged.astype(BF16), wo_ref[...])
    o_ref[...] = x_ref[...] + _rms_scale(mix, nwo_ref[...])


def _mix_merge(x2, ys, zs, ya, gs, nws, ws, wa, wo, nwo, *, tm=512):
    T, D = x2.shape
    tok = lambda w: pl.BlockSpec((tm, w), lambda i: (i, 0))
    return pl.pallas_call(
        _mix_merge_kernel,
        grid=(T // tm,),
        in_specs=[tok(D), tok(ys.shape[1]), tok(zs.shape[1]), tok(ya.shape[1]), tok(gs.shape[1]),
                  _full(nws.shape), _full(ws.shape), _full(wa.shape), _full(wo.shape),
                  _full(nwo.shape)],
        out_specs=tok(D),
        out_shape=jax.ShapeDtypeStruct((T, D), F32),
        compiler_params=_cparams(1),
        name="mix_merge",
    )(x2, ys, zs, ya, gs, nws, ws, wa, wo, nwo)


def _ffn_kernel(x_ref, nwi_ref, wg_ref, wu_ref, wd_ref, nwo_ref, o_ref, hn_ref, act_ref, *, th):
    x = x_ref[...]
    hn_ref[...] = _rms_scale(x, nwi_ref[...]).astype(BF16)
    for c in range(wg_ref.shape[1] // th):
        sl = slice(c * th, (c + 1) * th)
        gt = _dot(hn_ref[...], wg_ref[:, sl])
        up = _dot(hn_ref[...], wu_ref[:, sl])
        act_ref[:, sl] = (gt * _sigmoid(gt) * up).astype(BF16)
    y = _dot(act_ref[...], wd_ref[...])
    o_ref[...] = x + _rms_scale(y, nwo_ref[...])


def _ffn(x2, nwi, wg, wu, wd, nwo, *, tm=512, th=256):
    T, D = x2.shape
    Hf = wg.shape[1]
    tok = pl.BlockSpec((tm, D), lambda i: (i, 0))
    return pl.pallas_call(
        functools.partial(_ffn_kernel, th=th),
        grid=(T // tm,),
        in_specs=[tok, _full(nwi.shape), _full(wg.shape), _full(wu.shape), _full(wd.shape),
                  _full(nwo.shape)],
        out_specs=tok,
        out_shape=jax.ShapeDtypeStruct((T, D), F32),
        scratch_shapes=[pltpu.VMEM((tm, D), BF16), pltpu.VMEM((tm, Hf), BF16)],
        compiler_params=_cparams(1),
        name="ffn",
    )(x2, nwi, wg, wu, wd, nwo)


def _rope_tables(S):
    pos = jnp.arange(S, dtype=F32)
    inv_freq = ROPE_THETA ** (-jnp.arange(0, ROPE_DIM, 2, dtype=F32) / ROPE_DIM)
    ang = pos[:, None] * inv_freq[None, :]
    h = ROPE_DIM // 2
    cos, sin = jnp.cos(ang), jnp.sin(ang)
    pad = jnp.zeros((S, ATT_HEAD_DIM - ROPE_DIM), F32)
    zh = jnp.zeros((S, h), F32)
    cos_h = jnp.concatenate([cos, cos, pad + 1.0], axis=1)
    sa_h = jnp.concatenate([zh, sin, pad], axis=1)
    sb_h = jnp.concatenate([-sin, zh, pad], axis=1)
    rep = LANES // ATT_HEAD_DIM
    return tuple(jnp.tile(t, (1, rep)) for t in (cos_h, sa_h, sb_h))


def _attn_bias():
    i = np.arange(ATT_BQ)[None, :]
    j = np.arange(ATT_KW)[:, None]
    v = np.arange(3)[:, None, None]
    ok = np.abs(j - i - ATT_HALF * v) <= ATT_HALF
    return jnp.asarray(np.where(ok, 0.0, NEG_BIG), dtype=BF16)


def _layer(x, nmp, w_in, conv_w, conv_b, dt_bias, a_log, d_skip, ssd_nw, w_ssd, w_att, w_out,
           nmpost, nfpre, w_fin, w_fdown, nfpost):
    B, S, D = x.shape
    G, J = SSD_N_GROUPS, SSD_HEADS_PER_GROUP
    row = lambda v: v.reshape(1, -1).astype(F32)
    c0 = SSD_D_INNER
    c1 = c0 + SSD_D_INNER + 2 * SSD_BC_WIDTH
    c2 = c1 + 2 * SSD_N_HEADS
    c3 = c2 + 3 * ATT_WIDTH
    w16 = w_in.astype(BF16)
    perm = lambda v: v.reshape(v.shape[:-1] + (2, G, J)).swapaxes(-3, -2).reshape(v.shape[:-1] + (2 * G * J,))
    wdt = perm(w16[:, c1:c2])
    wdt = jnp.concatenate([wdt, jnp.zeros((D, LANES - wdt.shape[1]), wdt.dtype)], axis=1)
    wqkv = w16[:, c2:c3]
    wgt = w16[:, c3:]
    bias_col = perm(dt_bias.reshape(-1)).reshape(-1, 1).astype(F32)
    alog_col = perm(a_log.reshape(-1)).reshape(-1, 1).astype(F32)
    nw = row(nmp)

    zs, gs, dt_t, da_t = _proj_gate(x, nw, w16, wgt, wdt, bias_col, alog_col)
    cw8 = jnp.concatenate([conv_w, conv_b[None, :],
                           jnp.zeros((8 - SSD_CONV_WIDTH - 1, conv_w.shape[1]), F32)], axis=0)
    xs, bm, cm = _proj_xbc(x, nw, w16, cw8)
    qkv = _proj_qkv(x, nw, wqkv, *_rope_tables(S))

    dsk = jnp.repeat(d_skip.astype(F32), SSD_HEAD_DIM).reshape(G, 1, SSD_GROUP_WIDTH)
    y_ssd = _ssd_scan(xs, bm, cm, dt_t, da_t, dsk)
    y_att = _dil_attn(qkv, _attn_bias())

    T = B * S
    x1 = _mix_merge(x.reshape(T, D), y_ssd.reshape(T, -1), zs.reshape(T, -1), y_att.reshape(T, -1),
                    gs.reshape(T, -1), row(ssd_nw), w_ssd.astype(BF16), w_att.astype(BF16),
                    w_out.astype(BF16), row(nmpost))
    x2 = _ffn(x1, row(nfpre), w_fin[:, :FFN_HIDDEN].astype(BF16), w_fin[:, FFN_HIDDEN:].astype(BF16),
              w_fdown.astype(BF16), row(nfpost))
    return x2.reshape(B, S, D)


def kernel(x, norm_mix_pre, w_in, ssd_conv_w, ssd_conv_b, ssd_dt_bias, ssd_A_log, ssd_D, ssd_norm_w,
           w_ssd_branch, w_attn_branch, w_out, norm_mix_post, norm_ffn_pre, w_ffn_in, w_ffn_down,
           norm_ffn_post):
    for l in range(w_in.shape[0]):
        x = _layer(x, norm_mix_pre[l], w_in[l], ssd_conv_w[l], ssd_conv_b[l], ssd_dt_bias[l],
                   ssd_A_log[l], ssd_D[l], ssd_norm_w[l], w_ssd_branch[l], w_attn_branch[l], w_out[l],
                   norm_mix_post[l], norm_ffn_pre[l], w_ffn_in[l], w_ffn_down[l], norm_ffn_post[l])
    return x
```

```python
import functools

import jax
import jax.numpy as jnp
import numpy as np
from jax import lax
from jax.experimental import pallas as pl
from jax.experimental.pallas import tpu as pltpu

F32 = jnp.float32
BF16 = jnp.bfloat16

NORM_EPS = 1e-6
D_MODEL = 1024
SSD_D_INNER = 2048
SSD_HEAD_DIM = 64
SSD_N_GROUPS = 8
SSD_HEADS_PER_GROUP = 4
SSD_N_HEADS = SSD_N_GROUPS * SSD_HEADS_PER_GROUP
SSD_D_STATE = 128
SSD_GROUP_WIDTH = SSD_HEADS_PER_GROUP * SSD_HEAD_DIM
SSD_BC_WIDTH = SSD_N_GROUPS * SSD_D_STATE
SSD_CONV_WIDTH = 5
SSD_CHUNK = 128
SSD_UNROLL = 4
ATT_HEAD_DIM = 64
ATT_HEADS_PER_GROUP = 8
ATT_PATTERNS = ((128, 1), (512, 4), (2048, 16))
ATT_DILS = tuple(d for _, d in ATT_PATTERNS)
ATT_HALF = 64
assert all(w // (2 * d) == ATT_HALF for w, d in ATT_PATTERNS)
ATT_GROUP_WIDTH = ATT_HEADS_PER_GROUP * ATT_HEAD_DIM
ATT_WIDTH = len(ATT_PATTERNS) * ATT_GROUP_WIDTH
ROPE_THETA = 500000.0
ROPE_DIM = ATT_HEAD_DIM // 4
NEG_BIG = -1e30
LOG2E = 1.4426950408889634
FFN_HIDDEN = 2816
LANES = 128
SUBLANES = 8
CONV_HALO = 16
CONV_ROWS = 512
ATT_BQ = 128
ATT_KW = ATT_BQ + 2 * ATT_HALF
ATT_UNROLL = 8
VMEM_LIMIT = 56 * 1024 * 1024


def _cparams(n_axes):
    return pltpu.CompilerParams(
        dimension_semantics=("parallel",) * n_axes, vmem_limit_bytes=VMEM_LIMIT)


def _rms_scale(x, w):
    ms = jnp.mean(x * x, axis=-1, keepdims=True)
    return x * lax.rsqrt(ms + NORM_EPS) * w


def _sigmoid(a):
    return 1.0 / (1.0 + jnp.exp(-a))


def _softplus(a):
    return jnp.maximum(a, 0.0) + jnp.log(1.0 + jnp.exp(-jnp.abs(a)))


def _dot(a, b):
    return jnp.dot(a, b, preferred_element_type=F32)


def _dot_nt(a, b):
    return lax.dot_general(a, b, (((1,), (1,)), ((), ())), preferred_element_type=F32)


def _dot_tn(a, b):
    return lax.dot_general(a, b, (((0,), (0,)), ((), ())), preferred_element_type=F32)


def _full(shape):
    n = len(shape)
    return pl.BlockSpec(shape, lambda *_: (0,) * n, pipeline_mode=pl.Buffered(1))


def _cols(rows, width, block):
    return pl.BlockSpec((rows, width), lambda *_: (0, block), pipeline_mode=pl.Buffered(1))


def _head_selectors(rows):
    lo = (lax.broadcasted_iota(jnp.int32, (rows, LANES), 1) < LANES // 2).astype(F32)
    return lo.astype(BF16), (1.0 - lo).astype(BF16)


def _proj_gate_kernel(x_ref, nw_ref, wz_ref, wg_ref, wdt_ref, bias_ref, alog_ref,
                      zs_ref, gs_ref, v_ref, etot_ref, wst_ref, qc_ref, hn_ref, *, tn):
    hn_ref[...] = _rms_scale(x_ref[0], nw_ref[...]).astype(BF16)
    d = _dot(hn_ref[...], wdt_ref[...])
    n_dt = v_ref.shape[1]
    dt = _softplus(d.T[:n_dt, :] + bias_ref[...])
    da = dt * (-jnp.exp(alog_ref[...]))
    L = SSD_CHUNK
    tm = da.shape[1]
    lane_in_chunk = lax.broadcasted_iota(jnp.int32, da.shape, 1) & (L - 1)
    pre, suf = da, da
    k = 1
    while k < L:
        pre = pre + jnp.where(lane_in_chunk >= k, pltpu.roll(pre, k, 1), 0.0)
        suf = suf + jnp.where(lane_in_chunk < L - k, pltpu.roll(suf, tm - k, 1), 0.0)
        k *= 2
    tot = pre + suf - da
    row = lax.broadcasted_iota(jnp.int32, da.shape, 0)
    is_fwd = (row & (2 * SSD_HEADS_PER_GROUP - 1)) < SSD_HEADS_PER_GROUP
    a = jnp.where(is_fwd, pre, suf)
    a2 = a * LOG2E
    ea = jnp.exp(a)
    v_ref[0] = a2 - jnp.log2(dt)
    etot_ref[0] = jnp.exp(tot)
    wst_ref[0] = jnp.exp(tot - a) * dt
    qc_ref[0] = jnp.concatenate([a2, ea], axis=0).T

    for c in range(wz_ref.shape[1] // tn):
        sl = slice(c * tn, (c + 1) * tn)
        acc = _dot(hn_ref[...], wz_ref[:, sl])
        zs_ref[0, :, sl] = (acc * _sigmoid(acc)).astype(BF16)
    for c in range(wg_ref.shape[1] // tn):
        sl = slice(c * tn, (c + 1) * tn)
        acc = _dot(hn_ref[...], wg_ref[:, sl])
        gs_ref[0, :, sl] = _sigmoid(acc).astype(BF16)


def _proj_gate(x, nw, w16, wg, wdt, bias_col, alog_col, *, tm=512, tn=256):
    B, S, D = x.shape
    n_dt = bias_col.shape[0]
    nz = SSD_D_INNER
    grid = (B, S // tm)
    tok = lambda w: pl.BlockSpec((1, tm, w), lambda b, i: (b, i, 0))
    return pl.pallas_call(
        functools.partial(_proj_gate_kernel, tn=tn),
        grid=grid,
        in_specs=[tok(D), _full(nw.shape), _cols(D, nz, 0), _full(wg.shape), _full(wdt.shape),
                  _full(bias_col.shape), _full(alog_col.shape)],
        out_specs=[tok(nz), tok(wg.shape[1])]
        + [pl.BlockSpec((1, n_dt, tm), lambda b, i: (b, 0, i))] * 3 + [tok(2 * n_dt)],
        out_shape=[jax.ShapeDtypeStruct((B, S, nz), BF16),
                   jax.ShapeDtypeStruct((B, S, wg.shape[1]), BF16)]
        + [jax.ShapeDtypeStruct((B, n_dt, S), F32)] * 3
        + [jax.ShapeDtypeStruct((B, S, 2 * n_dt), F32)],
        scratch_shapes=[pltpu.VMEM((tm, D), BF16)],
        compiler_params=_cparams(2),
        name="proj_gate",
    )(x, nw, w16, wg, wdt, bias_col, alog_col)


def _proj_xbc_kernel(xp_ref, x_ref, xn_ref, nw_ref, wx_ref, wb_ref, wc_ref, cw_ref,
                     xs_ref, bm_ref, cm_ref, hn_ref, hh_ref, p_ref, f_ref, q_ref, *, tm):
    it = pl.program_id(1)
    H = CONV_HALO
    D = x_ref.shape[2]
    W = SSD_GROUP_WIDTH
    SL = SUBLANES
    BLK = SL * SL
    nb = tm // BLK
    GB = CONV_ROWS // BLK
    nw = nw_ref[...]
    xn = _rms_scale(x_ref[0], nw)
    for p in range(D // LANES):
        p_ref[p] = xn[:, p * LANES:(p + 1) * LANES]
    parts = []
    for i in range(SL):
        rows = jnp.concatenate([p_ref[p, pl.ds(i, tm // SL, stride=SL), :]
                                for p in range(D // LANES)], axis=1)
        parts.append(rows.reshape(nb, SL, D))
    hn_ref[...] = jnp.stack(parts, axis=1).reshape(tm, D).astype(BF16)
    keep_p = jnp.where(it > 0, 1.0, 0.0)
    keep_n = jnp.where(it < pl.num_programs(1) - 1, 1.0, 0.0)
    hh_ref[0:H] = (_rms_scale(xp_ref[0], nw) * keep_p).astype(BF16)
    hh_ref[H:] = (_rms_scale(xn_ref[0], nw) * keep_n).astype(BF16)

    sub = lax.broadcasted_iota(jnp.int32, (1, SL, W), 1)
    up = lambda v: pltpu.roll(v, SL - 1, v.ndim - 2)
    down = lambda v: pltpu.roll(v, 1, v.ndim - 2)

    def conv_silu(w, col0, store):
        f_ref[...] = _dot(hn_ref[...], w)
        halo = _dot(hh_ref[...], w)
        prev_tail, next_head = halo[H - SL:H], halo[H:H + SL]

        def vreg(b, i):
            return f_ref[b * BLK + i * SL:b * BLK + (i + 1) * SL, :]

        for gb in range(nb // GB):
            blocks = range(gb * GB, (gb + 1) * GB)
            a = f_ref[gb * CONV_ROWS:(gb + 1) * CONV_ROWS, :].reshape(GB, SL, SL, W)
            n0 = jnp.stack([vreg(b + 1, 0) if b + 1 < nb else next_head for b in blocks])
            n1 = jnp.stack([vreg(b + 1, 1) if b + 1 < nb else up(next_head) for b in blocks])
            p7 = jnp.stack([vreg(b - 1, SL - 1) if b > 0 else prev_tail for b in blocks])
            p6 = jnp.stack([vreg(b - 1, SL - 2) if b > 0 else down(prev_tail) for b in blocks])
            wp0 = jnp.where(sub < SL - 1, up(a[:, 0]), up(n0))[:, None]
            wp1 = jnp.where(sub < SL - 1, up(a[:, 1]), up(n1))[:, None]
            wm7 = jnp.where(sub >= 1, down(a[:, SL - 1]), down(p7))[:, None]
            wm6 = jnp.where(sub >= 1, down(a[:, SL - 2]), down(p6))[:, None]
            taps = (jnp.concatenate([wm6, wm7, a[:, :SL - 2]], axis=1),
                    jnp.concatenate([wm7, a[:, :SL - 1]], axis=1),
                    a,
                    jnp.concatenate([a[:, 1:], wp0], axis=1),
                    jnp.concatenate([a[:, 2:], wp0, wp1], axis=1))
            y = cw_ref[SSD_CONV_WIDTH:SSD_CONV_WIDTH + 1, col0:col0 + W]
            for k in range(SSD_CONV_WIDTH):
                y = y + cw_ref[k:k + 1, col0:col0 + W] * taps[k]
            y = (y * _sigmoid(y)).reshape(CONV_ROWS, W)
            for p in range(W // LANES):
                q_ref[p] = y[:, p * LANES:(p + 1) * LANES]
            nat = jnp.stack(
                [jnp.concatenate([q_ref[p, pl.ds(j, CONV_ROWS // SL, stride=SL), :]
                                  for p in range(W // LANES)], axis=1).reshape(GB, SL, W)
                 for j in range(SL)], axis=1)
            store(gb * CONV_ROWS, nat.reshape(CONV_ROWS, W).astype(BF16))

    for g in range(SSD_N_GROUPS):
        def store_x(r, y, g=g):
            xs_ref[0, g, r:r + CONV_ROWS, :] = y
        conv_silu(wx_ref[:, g * W:(g + 1) * W], g * W, store_x)
    for ref, w_ref, base in ((bm_ref, wb_ref, SSD_D_INNER), (cm_ref, wc_ref, SSD_D_INNER + SSD_BC_WIDTH)):
        for p in range(SSD_BC_WIDTH // W):
            def store_bc(r, y, ref=ref, p=p):
                ref[0, 2 * p, r:r + CONV_ROWS, :] = y[:, :SSD_D_STATE]
                ref[0, 2 * p + 1, r:r + CONV_ROWS, :] = y[:, SSD_D_STATE:]
            conv_silu(w_ref[:, p * W:(p + 1) * W], base + p * W, store_bc)


def _proj_xbc(x, nw, w16, cw8, *, tm=512):
    B, S, D = x.shape
    H = CONV_HALO
    G = SSD_N_GROUPS
    nh = S // H
    r = tm // H
    grid = (B, S // tm)
    x_blk = 1
    b_blk = 2 * SSD_D_INNER // SSD_BC_WIDTH
    c_blk = b_blk + 1
    return pl.pallas_call(
        functools.partial(_proj_xbc_kernel, tm=tm),
        grid=grid,
        in_specs=[pl.BlockSpec((1, H, D), lambda b, i: (b, jnp.maximum(i * r - 1, 0), 0)),
                  pl.BlockSpec((1, tm, D), lambda b, i: (b, i, 0)),
                  pl.BlockSpec((1, H, D), lambda b, i: (b, jnp.minimum((i + 1) * r, nh - 1), 0)),
                  _full(nw.shape), _cols(D, SSD_D_INNER, x_blk), _cols(D, SSD_BC_WIDTH, b_blk),
                  _cols(D, SSD_BC_WIDTH, c_blk), _full(cw8.shape)],
        out_specs=[pl.BlockSpec((1, G, tm, SSD_GROUP_WIDTH), lambda b, i: (b, 0, i, 0)),
                   pl.BlockSpec((1, G, tm, SSD_D_STATE), lambda b, i: (b, 0, i, 0)),
                   pl.BlockSpec((1, G, tm, SSD_D_STATE), lambda b, i: (b, 0, i, 0))],
        out_shape=[jax.ShapeDtypeStruct((B, G, S, SSD_GROUP_WIDTH), BF16),
                   jax.ShapeDtypeStruct((B, G, S, SSD_D_STATE), BF16),
                   jax.ShapeDtypeStruct((B, G, S, SSD_D_STATE), BF16)],
        scratch_shapes=[pltpu.VMEM((tm, D), BF16), pltpu.VMEM((2 * H, D), BF16),
                        pltpu.VMEM((D // LANES, tm, LANES), F32),
                        pltpu.VMEM((tm, SSD_GROUP_WIDTH), F32),
                        pltpu.VMEM((SSD_GROUP_WIDTH // LANES, CONV_ROWS, LANES), F32)],
        compiler_params=_cparams(2),
        name="proj_xbc",
    )(x, x, x, nw, w16, w16, w16, cw8)


def _proj_qkv_kernel(x_ref, nw_ref, w_ref, cos_ref, sa_ref, sb_ref, *rest, tm):
    outs, (hn_ref, f_ref) = rest[:9], rest[9:]
    hn_ref[...] = _rms_scale(x_ref[0], nw_ref[...]).astype(BF16)
    W = ATT_GROUP_WIDTH
    for typ in range(3):
        for g, d in enumerate(ATT_DILS):
            c = typ * len(ATT_DILS) + g
            a = _dot(hn_ref[...], w_ref[:, c * W:(c + 1) * W])
            if typ < 2:
                parts = []
                for p in range(W // LANES):
                    ch = a[:, p * LANES:(p + 1) * LANES]
                    parts.append(ch * cos_ref[...]
                                 + pltpu.roll(ch, ROPE_DIM // 2, 1) * sa_ref[...]
                                 + pltpu.roll(ch, LANES - ROPE_DIM // 2, 1) * sb_ref[...])
                a = jnp.concatenate(parts, axis=1)
            if typ == 0:
                a = a * (ATT_HEAD_DIM ** -0.5 * LOG2E)
            o_ref = outs[c]
            if d == 1:
                for p in range(W // LANES):
                    o_ref[0, 0, p] = a[:, p * LANES:(p + 1) * LANES].astype(BF16)
            else:
                for p in range(W // LANES):
                    f_ref[p] = a[:, p * LANES:(p + 1) * LANES]
                for r in range(d):
                    for p in range(W // LANES):
                        o_ref[0, r, p] = f_ref[p, pl.ds(r, tm // d, stride=d), :].astype(BF16)


def _proj_qkv(x, nw, wqkv, cos_t, sa_t, sb_t, *, tm=512):
    B, S, D = x.shape
    W = ATT_GROUP_WIDTH
    grid = (B, S // tm)
    tab = pl.BlockSpec((tm, LANES), lambda b, i: (i, 0))
    out_specs, out_shape = [], []
    for _ in range(3):
        for d in ATT_DILS:
            out_specs.append(pl.BlockSpec((1, d, W // LANES, tm // d, LANES),
                                          lambda b, i: (b, 0, 0, i, 0)))
            out_shape.append(jax.ShapeDtypeStruct((B, d, W // LANES, S // d, LANES), BF16))
    return pl.pallas_call(
        functools.partial(_proj_qkv_kernel, tm=tm),
        grid=grid,
        in_specs=[pl.BlockSpec((1, tm, D), lambda b, i: (b, i, 0)), _full(nw.shape),
                  _full(wqkv.shape), tab, tab, tab],
        out_specs=out_specs,
        out_shape=out_shape,
        scratch_shapes=[pltpu.VMEM((tm, D), BF16), pltpu.VMEM((W // LANES, tm, LANES), F32)],
        compiler_params=_cparams(2),
        name="proj_qkv",
    )(x, nw, wqkv, cos_t, sa_t, sb_t)


def _ssd_scan_kernel(xs_ref, bm_ref, cm_ref, v_ref, etot_ref, wst_ref, qc_ref, dsk_ref, y_ref,
                     st_ref, yacc_ref, *, S):
    L = SSD_CHUNK
    nc = S // L
    J = SSD_HEADS_PER_GROUP
    n_hd = qc_ref.shape[2] // 2
    g0 = pl.program_id(1) * (2 * J)
    st_ref[...] = jnp.zeros(st_ref.shape, F32)

    lo = lax.broadcasted_iota(jnp.int32, (L, LANES), 1) < SSD_HEAD_DIM
    lo_row = lo[0:1]
    head_col = jnp.where(lo, 0, 1)
    zero_col = jnp.zeros((L, LANES), jnp.int32)
    sel = _head_selectors(L)

    ti = lax.broadcasted_iota(jnp.int32, (L, L), 0)
    si = lax.broadcasted_iota(jnp.int32, (L, L), 1)
    causal = (ti >= si, si >= ti)
    dsk = dsk_ref[0]

    def chunk(c, d):
        t0 = pl.multiple_of(c * L, L)
        xc = xs_ref[0, 0, pl.ds(t0, L), :]
        bc = bm_ref[0, 0, pl.ds(t0, L), :]
        cc = cm_ref[0, 0, pl.ds(t0, L), :]
        cb = _dot_nt(cc, bc)
        qc = qc_ref[0, pl.ds(t0, L), :]
        vv = v_ref[0, :, pl.ds(t0, L)]
        ww = wst_ref[0, :, pl.ds(t0, L)]
        bt = bc.T
        ms, bw = [], []
        for j in range(J):
            r = J * d + j
            seg = jnp.take_along_axis(qc, zero_col + (g0 + r), axis=1) - vv[r:r + 1, :]
            dec = jnp.exp2(jnp.where(causal[d], seg, -jnp.inf))
            ms.append((cb * dec).astype(BF16))
            bw.append(bt * ww[r:r + 1, :].astype(BF16))
        yds, dss = [], []
        for p in range(J // 2):
            xp = xc[:, p * LANES:(p + 1) * LANES]
            rhs = jnp.concatenate([xp * sel[0], xp * sel[1]], axis=0)
            yds.append(_dot(jnp.concatenate([ms[2 * p], ms[2 * p + 1]], axis=1), rhs))
            dss.append(_dot(jnp.concatenate([bw[2 * p], bw[2 * p + 1]], axis=1), rhs))
        yd = jnp.concatenate(yds, axis=1)
        ds = jnp.concatenate(dss, axis=1)

        escale = jnp.concatenate(
            [jnp.take_along_axis(qc, head_col + (n_hd + g0 + J * d + 2 * p), axis=1)
             for p in range(J // 2)], axis=1)
        st = st_ref[d]
        yo = _dot(cc, st.astype(BF16)) * escale
        et = etot_ref[0, J * d:J * d + J, pl.ds(t0, L)]
        drow = jnp.concatenate([jnp.where(lo_row, et[2 * p:2 * p + 1], et[2 * p + 1:2 * p + 2])
                                for p in range(J // 2)], axis=1)
        st_ref[d] = st * drow + ds
        return t0, yd + yo, xc.astype(F32)

    U = SSD_UNROLL

    def steps(i):
        return [(c, d) for u in range(U) for c, d in ((i * U + u, 0), (nc - 1 - i * U - u, 1))]

    def first_touch(i, carry):
        for c, d in steps(i):
            t0, y, _ = chunk(c, d)
            yacc_ref[pl.ds(t0, L), :] = y
        return carry

    def second_touch(i, carry):
        for c, d in steps(i):
            t0, y, xf = chunk(c, d)
            y_ref[0, 0, pl.ds(t0, L), :] = (yacc_ref[pl.ds(t0, L), :] + y + xf * dsk).astype(BF16)
        return carry

    lax.fori_loop(0, nc // 2 // U, first_touch, 0)
    lax.fori_loop(nc // 2 // U, nc // U, second_touch, 0)


def _ssd_scan(xs, bm, cm, v_t, etot_t, wst_t, qc, dsk):
    B, G, S, W = xs.shape
    N = SSD_D_STATE
    J2 = 2 * SSD_HEADS_PER_GROUP
    assert (S // SSD_CHUNK) % (2 * SSD_UNROLL) == 0
    grp = lambda w: pl.BlockSpec((1, 1, S, w), lambda b, g: (b, g, 0, 0))
    rows = pl.BlockSpec((1, J2, S), lambda b, g: (b, g, 0))
    return pl.pallas_call(
        functools.partial(_ssd_scan_kernel, S=S),
        grid=(B, G),
        in_specs=[grp(W), grp(N), grp(N), rows, rows, rows,
                  pl.BlockSpec((1, S, qc.shape[2]), lambda b, g: (b, 0, 0)),
                  pl.BlockSpec((1, 1, W), lambda b, g: (g, 0, 0))],
        out_specs=pl.BlockSpec((1, 1, S, W), lambda b, g: (b, g, 0, 0)),
        out_shape=jax.ShapeDtypeStruct((B, G, S, W), BF16),
        scratch_shapes=[pltpu.VMEM((2, N, W), F32), pltpu.VMEM((S, W), F32)],
        compiler_params=_cparams(2),
        name="ssd_scan",
    )(xs, bm, cm, v_t, etot_t, wst_t, qc, dsk)


def _dil_attn_kernel(*refs, S):
    qkv_refs, bias_ref, o_ref, vm_ref, scr = refs[:9], refs[9], refs[10], refs[11], refs[12:]
    BQ, KW, HD = ATT_BQ, ATT_KW, ATT_HEAD_DIM
    lo_q = lax.broadcasted_iota(jnp.int32, (BQ, LANES), 1) < HD
    sel_q = _head_selectors(BQ)
    eye_q = (lax.broadcasted_iota(jnp.int32, (BQ, BQ), 0)
             == lax.broadcasted_iota(jnp.int32, (BQ, BQ), 1)).astype(F32).astype(BF16)
    sel_k = _head_selectors(KW)

    def split_values(v_ref, n):
        def body(c, carry):
            row = pl.multiple_of(c * KW, KW)
            v = v_ref[0, row // n, 0, pl.ds(row % n, KW), :]
            for h in range(2):
                vm_ref[h, pl.ds(row, KW), :] = v * sel_k[h]
            return carry
        lax.fori_loop(0, S // KW, body, 0)

    def attend(qb, kwin, va, vb, bias_t):
        k_aug = jnp.concatenate([kwin, bias_t], axis=1)
        q_aug = jnp.concatenate(
            [jnp.concatenate([qb * sel_q[h], eye_q], axis=1) for h in range(2)], axis=0)
        s = _dot_nt(q_aug, k_aug)
        m = jnp.max(s, axis=-1, keepdims=True)
        p = jnp.exp2(s - m).astype(BF16)
        v_aug = jnp.concatenate([jnp.concatenate([va, sel_k[0]], axis=1),
                                 jnp.concatenate([vb, sel_k[1]], axis=1)], axis=0)
        accl = _dot(jnp.concatenate([p[:BQ], p[BQ:]], axis=1), v_aug)
        return accl[:, :LANES], jnp.where(lo_q, m[:BQ], m[BQ:]), accl[:, LANES:]

    def block(q_ref, k_ref, n, it):
        nblk = n // BQ
        r = it // nblk
        qs = pl.multiple_of((it % nblk) * BQ, BQ)
        ks = pl.multiple_of(jnp.clip(qs - ATT_HALF, 0, n - KW), ATT_HALF)
        variant = (qs - ks) // ATT_HALF
        vrow = pl.multiple_of(r * n + ks, ATT_HALF)
        parts = attend(q_ref[0, r, 0, pl.ds(qs, BQ), :], k_ref[0, r, 0, pl.ds(ks, KW), :],
                       vm_ref[0, pl.ds(vrow, KW), :], vm_ref[1, pl.ds(vrow, KW), :],
                       bias_ref[variant])
        return r, qs, parts

    for g in range(1, len(ATT_DILS)):
        d = ATT_DILS[g]
        q_ref, k_ref, v_ref = qkv_refs[g], qkv_refs[3 + g], qkv_refs[6 + g]
        g_scr = scr[3 * (g - 1):3 * g]
        split_values(v_ref, S // d)

        def dil_body(i, carry, d=d, q_ref=q_ref, k_ref=k_ref, g_scr=g_scr):
            for u in range(ATT_UNROLL):
                r, qs, parts = block(q_ref, k_ref, S // d, i * ATT_UNROLL + u)
                for ref, val in zip(g_scr, parts):
                    ref[pl.ds(r + d * qs, BQ, stride=d), :] = val
            return carry

        lax.fori_loop(0, S // BQ // ATT_UNROLL, dil_body, 0)

    split_values(qkv_refs[6], S)

    def merge_body(i, carry):
        for u in range(ATT_UNROLL):
            _, qs, parts = block(qkv_refs[0], qkv_refs[3], S, i * ATT_UNROLL + u)
            groups = [parts] + [tuple(ref[pl.ds(qs, BQ), :] for ref in scr[3 * (g - 1):3 * g])
                                for g in range(1, len(ATT_DILS))]
            m = functools.reduce(jnp.maximum, [gm for _, gm, _ in groups])
            ws = [jnp.exp2(gm - m) for _, gm, _ in groups]
            num = functools.reduce(lambda a, b: a + b, [w * ga for w, (ga, _, _) in zip(ws, groups)])
            den = functools.reduce(lambda a, b: a + b, [w * gd for w, (_, _, gd) in zip(ws, groups)])
            o_ref[0, 0, pl.ds(qs, BQ), :] = (num / den).astype(BF16)
        return carry

    lax.fori_loop(0, S // BQ // ATT_UNROLL, merge_body, 0)


def _dil_attn(qkv, bias):
    B = qkv[0].shape[0]
    S = qkv[0].shape[3]
    W = ATT_GROUP_WIDTH
    in_specs = []
    for t in range(3):
        for d in ATT_DILS:
            in_specs.append(pl.BlockSpec((1, d, 1, S // d, LANES), lambda b, p: (b, 0, p, 0, 0)))
    in_specs.append(_full(bias.shape))
    n_scr = 3 * (len(ATT_DILS) - 1)
    return pl.pallas_call(
        functools.partial(_dil_attn_kernel, S=S),
        grid=(B, W // LANES),
        in_specs=in_specs,
        out_specs=pl.BlockSpec((1, 1, S, LANES), lambda b, p: (b, p, 0, 0)),
        out_shape=jax.ShapeDtypeStruct((B, W // LANES, S, LANES), BF16),
        scratch_shapes=[pltpu.VMEM((2, S, LANES), BF16)] + [pltpu.VMEM((S, LANES), F32)] * n_scr,
        compiler_params=_cparams(2),
        name="dil_attn",
    )(*qkv, bias)


def _mix_merge_kernel(x_ref, ys_ref, zs_ref, ya_ref, gs_ref, nws_ref, ws_ref, wa_ref, wo_ref,
                      nwo_ref, o_ref):
    ys = jnp.concatenate([ys_ref[0, g] for g in range(ys_ref.shape[1])], axis=1)
    ya = jnp.concatenate([ya_ref[0, p] for p in range(ya_ref.shape[1])], axis=1)
    u = ys.astype(F32) * zs_ref[...].astype(F32)
    un = _rms_scale(u, nws_ref[...]).astype(BF16)
    a = _dot(un, ws_ref[...])
    b = _dot(ya, wa_ref[...])
    gs = gs_ref[...].astype(F32)
    merged = gs[:, :D_MODEL] * a + gs[:, D_MODEL:] * b
    mix = _dot(merged.astype(BF16), wo_ref[...])
    o_ref[...] = x_ref[...] + _rms_scale(mix, nwo_ref[...])


def _mix_merge(x2, ys, zs, ya, gs, nws, ws, wa, wo, nwo, *, tm=512):
    T, D = x2.shape
    nt = ys.shape[2] // tm
    tok = lambda w: pl.BlockSpec((tm, w), lambda i: (i, 0))
    slab = lambda a: pl.BlockSpec((1, a.shape[1], tm, a.shape[3]), lambda i: (i // nt, 0, i % nt, 0))
    return pl.pallas_call(
        _mix_merge_kernel,
        grid=(T // tm,),
        in_specs=[tok(D), slab(ys), tok(zs.shape[1]), slab(ya), tok(gs.shape[1]),
                  _full(nws.shape), _full(ws.shape), _full(wa.shape), _full(wo.shape),
                  _full(nwo.shape)],
        out_specs=tok(D),
        out_shape=jax.ShapeDtypeStruct((T, D), F32),
        compiler_params=_cparams(1),
        name="mix_merge",
    )(x2, ys, zs, ya, gs, nws, ws, wa, wo, nwo)


def _ffn_kernel(x_ref, nwi_ref, wg_ref, wu_ref, wd_ref, nwo_ref, o_ref, hn_ref, act_ref, *, th):
    x = x_ref[...]
    hn_ref[...] = _rms_scale(x, nwi_ref[...]).astype(BF16)
    for c in range(wg_ref.shape[1] // th):
        sl = slice(c * th, (c + 1) * th)
        gt = _dot(hn_ref[...], wg_ref[:, sl])
        up = _dot(hn_ref[...], wu_ref[:, sl])
        act_ref[:, sl] = (gt * _sigmoid(gt) * up).astype(BF16)
    y = _dot(act_ref[...], wd_ref[...])
    o_ref[...] = x + _rms_scale(y, nwo_ref[...])


def _ffn(x2, nwi, wg, wu, wd, nwo, *, tm=512, th=256):
    T, D = x2.shape
    Hf = wg.shape[1]
    tok = pl.BlockSpec((tm, D), lambda i: (i, 0))
    return pl.pallas_call(
        functools.partial(_ffn_kernel, th=th),
        grid=(T // tm,),
        in_specs=[tok, _full(nwi.shape), _full(wg.shape), _full(wu.shape), _full(wd.shape),
                  _full(nwo.shape)],
        out_specs=tok,
        out_shape=jax.ShapeDtypeStruct((T, D), F32),
        scratch_shapes=[pltpu.VMEM((tm, D), BF16), pltpu.VMEM((tm, Hf), BF16)],
        compiler_params=_cparams(1),
        name="ffn",
    )(x2, nwi, wg, wu, wd, nwo)


def _rope_tables(S):
    pos = jnp.arange(S, dtype=F32)
    inv_freq = ROPE_THETA ** (-jnp.arange(0, ROPE_DIM, 2, dtype=F32) / ROPE_DIM)
    ang = pos[:, None] * inv_freq[None, :]
    h = ROPE_DIM // 2
    cos, sin = jnp.cos(ang), jnp.sin(ang)
    pad = jnp.zeros((S, ATT_HEAD_DIM - ROPE_DIM), F32)
    zh = jnp.zeros((S, h), F32)
    cos_h = jnp.concatenate([cos, cos, pad + 1.0], axis=1)
    sa_h = jnp.concatenate([zh, sin, pad], axis=1)
    sb_h = jnp.concatenate([-sin, zh, pad], axis=1)
    rep = LANES // ATT_HEAD_DIM
    return tuple(jnp.tile(t, (1, rep)) for t in (cos_h, sa_h, sb_h))


def _attn_bias():
    i = np.arange(ATT_BQ)[None, :]
    j = np.arange(ATT_KW)[:, None]
    v = np.arange(3)[:, None, None]
    ok = np.abs(j - i - ATT_HALF * v) <= ATT_HALF
    return jnp.asarray(np.where(ok, 0.0, NEG_BIG), dtype=BF16)


def _layer(x, nmp, w_in, conv_w, conv_b, dt_bias, a_log, d_skip, ssd_nw, w_ssd, w_att, w_out,
           nmpost, nfpre, w_fin, w_fdown, nfpost):
    B, S, D = x.shape
    G, J = SSD_N_GROUPS, SSD_HEADS_PER_GROUP
    row = lambda v: v.reshape(1, -1).astype(F32)
    c0 = SSD_D_INNER
    c1 = c0 + SSD_D_INNER + 2 * SSD_BC_WIDTH
    c2 = c1 + 2 * SSD_N_HEADS
    c3 = c2 + 3 * ATT_WIDTH
    w16 = w_in.astype(BF16)
    perm = lambda v: v.reshape(v.shape[:-1] + (2, G, J)).swapaxes(-3, -2).reshape(v.shape[:-1] + (2 * G * J,))
    wdt = perm(w16[:, c1:c2])
    wdt = jnp.concatenate([wdt, jnp.zeros((D, LANES - wdt.shape[1]), wdt.dtype)], axis=1)
    wqkv = w16[:, c2:c3]
    wgt = w16[:, c3:]
    bias_col = perm(dt_bias.reshape(-1)).reshape(-1, 1).astype(F32)
    alog_col = perm(a_log.reshape(-1)).reshape(-1, 1).astype(F32)
    nw = row(nmp)

    zs, gs, v_t, etot_t, wst_t, qc = _proj_gate(x, nw, w16, wgt, wdt, bias_col, alog_col)
    cw8 = jnp.concatenate([conv_w, conv_b[None, :],
                           jnp.zeros((8 - SSD_CONV_WIDTH - 1, conv_w.shape[1]), F32)], axis=0)
    xs, bm, cm = _proj_xbc(x, nw, w16, cw8)
    qkv = _proj_qkv(x, nw, wqkv, *_rope_tables(S))

    dsk = jnp.repeat(d_skip.astype(F32), SSD_HEAD_DIM).reshape(G, 1, SSD_GROUP_WIDTH)
    y_ssd = _ssd_scan(xs, bm, cm, v_t, etot_t, wst_t, qc, dsk)
    y_att = _dil_attn(qkv, _attn_bias())

    T = B * S
    x1 = _mix_merge(x.reshape(T, D), y_ssd, zs.reshape(T, -1), y_att, gs.reshape(T, -1), row(ssd_nw), w_ssd.astype(BF16), w_att.astype(BF16),
                    w_out.astype(BF16), row(nmpost))
    x2 = _ffn(x1, row(nfpre), w_fin[:, :FFN_HIDDEN].astype(BF16), w_fin[:, FFN_HIDDEN:].astype(BF16),
              w_fdown.astype(BF16), row(nfpost))
    return x2.reshape(B, S, D)


def kernel(x, norm_mix_pre, w_in, ssd_conv_w, ssd_conv_b, ssd_dt_bias, ssd_A_log, ssd_D, ssd_norm_w,
           w_ssd_branch, w_attn_branch, w_out, norm_mix_post, norm_ffn_pre, w_ffn_in, w_ffn_down,
           norm_ffn_post):
    for l in range(w_in.shape[0]):
        x = _layer(x, norm_mix_pre[l], w_in[l], ssd_conv_w[l], ssd_conv_b[l], ssd_dt_bias[l],
                   ssd_A_log[l], ssd_D[l], ssd_norm_w[l], w_ssd_branch[l], w_attn_branch[l], w_out[l],
                   norm_mix_post[l], norm_ffn_pre[l], w_ffn_in[l], w_ffn_down[l], norm_ffn_post[l])
    return x
```

```python
import functools

import jax
import jax.numpy as jnp
import numpy as np
from jax import lax
from jax.experimental import pallas as pl
from jax.experimental.pallas import tpu as pltpu

F32 = jnp.float32
BF16 = jnp.bfloat16

NORM_EPS = 1e-6
D_MODEL = 1024
SSD_D_INNER = 2048
SSD_HEAD_DIM = 64
SSD_N_GROUPS = 8
SSD_HEADS_PER_GROUP = 4
SSD_N_HEADS = SSD_N_GROUPS * SSD_HEADS_PER_GROUP
SSD_D_STATE = 128
SSD_GROUP_WIDTH = SSD_HEADS_PER_GROUP * SSD_HEAD_DIM
SSD_BC_WIDTH = SSD_N_GROUPS * SSD_D_STATE
SSD_CONV_WIDTH = 5
SSD_CHUNK = 128
SSD_UNROLL = 4
ATT_HEAD_DIM = 64
ATT_HEADS_PER_GROUP = 8
ATT_PATTERNS = ((128, 1), (512, 4), (2048, 16))
ATT_DILS = tuple(d for _, d in ATT_PATTERNS)
ATT_HALF = 64
assert all(w // (2 * d) == ATT_HALF for w, d in ATT_PATTERNS)
ATT_GROUP_WIDTH = ATT_HEADS_PER_GROUP * ATT_HEAD_DIM
ATT_WIDTH = len(ATT_PATTERNS) * ATT_GROUP_WIDTH
ROPE_THETA = 500000.0
ROPE_DIM = ATT_HEAD_DIM // 4
NEG_BIG = -1e30
LOG2E = 1.4426950408889634
FFN_HIDDEN = 2816
LANES = 128
SUBLANES = 8
CONV_HALO = 16
CONV_ROWS = 512
ATT_BQ = 128
ATT_KW = ATT_BQ + 2 * ATT_HALF
ATT_UNROLL = 8
VMEM_LIMIT = 56 * 1024 * 1024


def _cparams(n_axes):
    return pltpu.CompilerParams(
        dimension_semantics=("parallel",) * n_axes, vmem_limit_bytes=VMEM_LIMIT)


def _rms_scale(x, w):
    ms = jnp.mean(x * x, axis=-1, keepdims=True)
    return x * lax.rsqrt(ms + NORM_EPS) * w


def _sigmoid(a):
    return 1.0 / (1.0 + jnp.exp(-a))


def _softplus(a):
    return jnp.maximum(a, 0.0) + jnp.log(1.0 + jnp.exp(-jnp.abs(a)))


def _dot(a, b):
    return jnp.dot(a, b, preferred_element_type=F32)


def _dot_nt(a, b):
    return lax.dot_general(a, b, (((1,), (1,)), ((), ())), preferred_element_type=F32)


def _dot_tn(a, b):
    return lax.dot_general(a, b, (((0,), (0,)), ((), ())), preferred_element_type=F32)


def _full(shape):
    n = len(shape)
    return pl.BlockSpec(shape, lambda *_: (0,) * n, pipeline_mode=pl.Buffered(1))


def _cols(rows, width, block):
    return pl.BlockSpec((rows, width), lambda *_: (0, block), pipeline_mode=pl.Buffered(1))


def _head_selectors(rows):
    lo = (lax.broadcasted_iota(jnp.int32, (rows, LANES), 1) < LANES // 2).astype(F32)
    return lo.astype(BF16), (1.0 - lo).astype(BF16)


def _proj_gate_kernel(x_ref, nw_ref, wz_ref, wg_ref, wdt_ref, bias_ref, alog_ref,
                      zs_ref, gs_ref, v_ref, etot_ref, wst_ref, qc_ref, hn_ref, *, tn):
    hn_ref[...] = _rms_scale(x_ref[0], nw_ref[...]).astype(BF16)
    d = _dot(hn_ref[...], wdt_ref[...])
    n_dt = v_ref.shape[1]
    dt = _softplus(d.T[:n_dt, :] + bias_ref[...])
    da = dt * (-jnp.exp(alog_ref[...]))
    L = SSD_CHUNK
    tm = da.shape[1]
    lane_in_chunk = lax.broadcasted_iota(jnp.int32, da.shape, 1) & (L - 1)
    pre, suf = da, da
    k = 1
    while k < L:
        pre = pre + jnp.where(lane_in_chunk >= k, pltpu.roll(pre, k, 1), 0.0)
        suf = suf + jnp.where(lane_in_chunk < L - k, pltpu.roll(suf, tm - k, 1), 0.0)
        k *= 2
    tot = pre + suf - da
    row = lax.broadcasted_iota(jnp.int32, da.shape, 0)
    is_fwd = (row & (2 * SSD_HEADS_PER_GROUP - 1)) < SSD_HEADS_PER_GROUP
    a = jnp.where(is_fwd, pre, suf)
    a2 = a * LOG2E
    ea = jnp.exp(a)
    v_ref[0] = a2 - jnp.log2(dt)
    etot_ref[0] = jnp.exp(tot)
    wst_ref[0] = jnp.exp(tot - a) * dt
    qc_ref[0] = jnp.concatenate([a2, ea], axis=0).T

    for c in range(wz_ref.shape[1] // tn):
        sl = slice(c * tn, (c + 1) * tn)
        acc = _dot(hn_ref[...], wz_ref[:, sl])
        zs_ref[0, :, sl] = (acc * _sigmoid(acc)).astype(BF16)
    for c in range(wg_ref.shape[1] // tn):
        sl = slice(c * tn, (c + 1) * tn)
        acc = _dot(hn_ref[...], wg_ref[:, sl])
        gs_ref[0, :, sl] = _sigmoid(acc).astype(BF16)


def _proj_gate(x, nw, w16, wg, wdt, bias_col, alog_col, *, tm=512, tn=256):
    B, S, D = x.shape
    n_dt = bias_col.shape[0]
    nz = SSD_D_INNER
    grid = (B, S // tm)
    tok = lambda w: pl.BlockSpec((1, tm, w), lambda b, i: (b, i, 0))
    return pl.pallas_call(
        functools.partial(_proj_gate_kernel, tn=tn),
        grid=grid,
        in_specs=[tok(D), _full(nw.shape), _cols(D, nz, 0), _full(wg.shape), _full(wdt.shape),
                  _full(bias_col.shape), _full(alog_col.shape)],
        out_specs=[tok(nz), tok(wg.shape[1])]
        + [pl.BlockSpec((1, n_dt, tm), lambda b, i: (b, 0, i))] * 3 + [tok(2 * n_dt)],
        out_shape=[jax.ShapeDtypeStruct((B, S, nz), BF16),
                   jax.ShapeDtypeStruct((B, S, wg.shape[1]), BF16)]
        + [jax.ShapeDtypeStruct((B, n_dt, S), F32)] * 3
        + [jax.ShapeDtypeStruct((B, S, 2 * n_dt), F32)],
        scratch_shapes=[pltpu.VMEM((tm, D), BF16)],
        compiler_params=_cparams(2),
        name="proj_gate",
    )(x, nw, w16, wg, wdt, bias_col, alog_col)


def _proj_xbc_kernel(xp_ref, x_ref, xn_ref, nw_ref, wx_ref, wb_ref, wc_ref, cw_ref,
                     xs_ref, bm_ref, cm_ref, hn_ref, hh_ref, p_ref, f_ref, q_ref, *, tm):
    it = pl.program_id(1)
    H = CONV_HALO
    D = x_ref.shape[2]
    W = SSD_GROUP_WIDTH
    SL = SUBLANES
    BLK = SL * SL
    nb = tm // BLK
    GB = CONV_ROWS // BLK
    nw = nw_ref[...]
    xn = _rms_scale(x_ref[0], nw)
    for p in range(D // LANES):
        p_ref[p] = xn[:, p * LANES:(p + 1) * LANES]
    parts = []
    for i in range(SL):
        rows = jnp.concatenate([p_ref[p, pl.ds(i, tm // SL, stride=SL), :]
                                for p in range(D // LANES)], axis=1)
        parts.append(rows.reshape(nb, SL, D))
    hn_ref[...] = jnp.stack(parts, axis=1).reshape(tm, D).astype(BF16)
    keep_p = jnp.where(it > 0, 1.0, 0.0)
    keep_n = jnp.where(it < pl.num_programs(1) - 1, 1.0, 0.0)
    hh_ref[0:H] = (_rms_scale(xp_ref[0], nw) * keep_p).astype(BF16)
    hh_ref[H:] = (_rms_scale(xn_ref[0], nw) * keep_n).astype(BF16)

    sub = lax.broadcasted_iota(jnp.int32, (1, SL, W), 1)
    up = lambda v: pltpu.roll(v, SL - 1, v.ndim - 2)
    down = lambda v: pltpu.roll(v, 1, v.ndim - 2)

    def conv_silu(w, col0, store):
        f_ref[...] = _dot(hn_ref[...], w)
        halo = _dot(hh_ref[...], w)
        prev_tail, next_head = halo[H - SL:H], halo[H:H + SL]

        def vreg(b, i):
            return f_ref[b * BLK + i * SL:b * BLK + (i + 1) * SL, :]

        for gb in range(nb // GB):
            blocks = range(gb * GB, (gb + 1) * GB)
            a = f_ref[gb * CONV_ROWS:(gb + 1) * CONV_ROWS, :].reshape(GB, SL, SL, W)
            n0 = jnp.stack([vreg(b + 1, 0) if b + 1 < nb else next_head for b in blocks])
            n1 = jnp.stack([vreg(b + 1, 1) if b + 1 < nb else up(next_head) for b in blocks])
            p7 = jnp.stack([vreg(b - 1, SL - 1) if b > 0 else prev_tail for b in blocks])
            p6 = jnp.stack([vreg(b - 1, SL - 2) if b > 0 else down(prev_tail) for b in blocks])
            wp0 = jnp.where(sub < SL - 1, up(a[:, 0]), up(n0))[:, None]
            wp1 = jnp.where(sub < SL - 1, up(a[:, 1]), up(n1))[:, None]
            wm7 = jnp.where(sub >= 1, down(a[:, SL - 1]), down(p7))[:, None]
            wm6 = jnp.where(sub >= 1, down(a[:, SL - 2]), down(p6))[:, None]
            taps = (jnp.concatenate([wm6, wm7, a[:, :SL - 2]], axis=1),
                    jnp.concatenate([wm7, a[:, :SL - 1]], axis=1),
                    a,
                    jnp.concatenate([a[:, 1:], wp0], axis=1),
                    jnp.concatenate([a[:, 2:], wp0, wp1], axis=1))
            y = cw_ref[SSD_CONV_WIDTH:SSD_CONV_WIDTH + 1, col0:col0 + W]
            for k in range(SSD_CONV_WIDTH):
                y = y + cw_ref[k:k + 1, col0:col0 + W] * taps[k]
            y = (y * _sigmoid(y)).reshape(CONV_ROWS, W)
            for p in range(W // LANES):
                q_ref[p] = y[:, p * LANES:(p + 1) * LANES]
            nat = jnp.stack(
                [jnp.concatenate([q_ref[p, pl.ds(j, CONV_ROWS // SL, stride=SL), :]
                                  for p in range(W // LANES)], axis=1).reshape(GB, SL, W)
                 for j in range(SL)], axis=1)
            store(gb * CONV_ROWS, nat.reshape(CONV_ROWS, W).astype(BF16))

    for g in range(SSD_N_GROUPS):
        def store_x(r, y, g=g):
            xs_ref[0, g, r:r + CONV_ROWS, :] = y
        conv_silu(wx_ref[:, g * W:(g + 1) * W], g * W, store_x)
    for ref, w_ref, base in ((bm_ref, wb_ref, SSD_D_INNER), (cm_ref, wc_ref, SSD_D_INNER + SSD_BC_WIDTH)):
        for p in range(SSD_BC_WIDTH // W):
            def store_bc(r, y, ref=ref, p=p):
                ref[0, 2 * p, r:r + CONV_ROWS, :] = y[:, :SSD_D_STATE]
                ref[0, 2 * p + 1, r:r + CONV_ROWS, :] = y[:, SSD_D_STATE:]
            conv_silu(w_ref[:, p * W:(p + 1) * W], base + p * W, store_bc)


def _proj_xbc(x, nw, w16, cw8, *, tm=512):
    B, S, D = x.shape
    H = CONV_HALO
    G = SSD_N_GROUPS
    nh = S // H
    r = tm // H
    grid = (B, S // tm)
    x_blk = 1
    b_blk = 2 * SSD_D_INNER // SSD_BC_WIDTH
    c_blk = b_blk + 1
    return pl.pallas_call(
        functools.partial(_proj_xbc_kernel, tm=tm),
        grid=grid,
        in_specs=[pl.BlockSpec((1, H, D), lambda b, i: (b, jnp.maximum(i * r - 1, 0), 0)),
                  pl.BlockSpec((1, tm, D), lambda b, i: (b, i, 0)),
                  pl.BlockSpec((1, H, D), lambda b, i: (b, jnp.minimum((i + 1) * r, nh - 1), 0)),
                  _full(nw.shape), _cols(D, SSD_D_INNER, x_blk), _cols(D, SSD_BC_WIDTH, b_blk),
                  _cols(D, SSD_BC_WIDTH, c_blk), _full(cw8.shape)],
        out_specs=[pl.BlockSpec((1, G, tm, SSD_GROUP_WIDTH), lambda b, i: (b, 0, i, 0)),
                   pl.BlockSpec((1, G, tm, SSD_D_STATE), lambda b, i: (b, 0, i, 0)),
                   pl.BlockSpec((1, G, tm, SSD_D_STATE), lambda b, i: (b, 0, i, 0))],
        out_shape=[jax.ShapeDtypeStruct((B, G, S, SSD_GROUP_WIDTH), BF16),
                   jax.ShapeDtypeStruct((B, G, S, SSD_D_STATE), BF16),
                   jax.ShapeDtypeStruct((B, G, S, SSD_D_STATE), BF16)],
        scratch_shapes=[pltpu.VMEM((tm, D), BF16), pltpu.VMEM((2 * H, D), BF16),
                        pltpu.VMEM((D // LANES, tm, LANES), F32),
                        pltpu.VMEM((tm, SSD_GROUP_WIDTH), F32),
                        pltpu.VMEM((SSD_GROUP_WIDTH // LANES, CONV_ROWS, LANES), F32)],
        compiler_params=_cparams(2),
        name="proj_xbc",
    )(x, x, x, nw, w16, w16, w16, cw8)


def _proj_qkv_kernel(x_ref, nw_ref, w_ref, cos_ref, sa_ref, sb_ref, *rest, tm):
    outs, (hn_ref, f_ref) = rest[:9], rest[9:]
    hn_ref[...] = _rms_scale(x_ref[0], nw_ref[...]).astype(BF16)
    W = ATT_GROUP_WIDTH
    for typ in range(3):
        for g, d in enumerate(ATT_DILS):
            c = typ * len(ATT_DILS) + g
            a = _dot(hn_ref[...], w_ref[:, c * W:(c + 1) * W])
            if typ < 2:
                parts = []
                for p in range(W // LANES):
                    ch = a[:, p * LANES:(p + 1) * LANES]
                    parts.append(ch * cos_ref[...]
                                 + pltpu.roll(ch, ROPE_DIM // 2, 1) * sa_ref[...]
                                 + pltpu.roll(ch, LANES - ROPE_DIM // 2, 1) * sb_ref[...])
                a = jnp.concatenate(parts, axis=1)
            if typ == 0:
                a = a * (ATT_HEAD_DIM ** -0.5 * LOG2E)
            o_ref = outs[c]
            if d == 1:
                for p in range(W // LANES):
                    o_ref[0, 0, p] = a[:, p * LANES:(p + 1) * LANES].astype(BF16)
            else:
                for p in range(W // LANES):
                    f_ref[p] = a[:, p * LANES:(p + 1) * LANES]
                for r in range(d):
                    for p in range(W // LANES):
                        o_ref[0, r, p] = f_ref[p, pl.ds(r, tm // d, stride=d), :].astype(BF16)


def _proj_qkv(x, nw, wqkv, cos_t, sa_t, sb_t, *, tm=512):
    B, S, D = x.shape
    W = ATT_GROUP_WIDTH
    grid = (B, S // tm)
    tab = pl.BlockSpec((tm, LANES), lambda b, i: (i, 0))
    out_specs, out_shape = [], []
    for _ in range(3):
        for d in ATT_DILS:
            out_specs.append(pl.BlockSpec((1, d, W // LANES, tm // d, LANES),
                                          lambda b, i: (b, 0, 0, i, 0)))
            out_shape.append(jax.ShapeDtypeStruct((B, d, W // LANES, S // d, LANES), BF16))
    return pl.pallas_call(
        functools.partial(_proj_qkv_kernel, tm=tm),
        grid=grid,
        in_specs=[pl.BlockSpec((1, tm, D), lambda b, i: (b, i, 0)), _full(nw.shape),
                  _full(wqkv.shape), tab, tab, tab],
        out_specs=out_specs,
        out_shape=out_shape,
        scratch_shapes=[pltpu.VMEM((tm, D), BF16), pltpu.VMEM((W // LANES, tm, LANES), F32)],
        compiler_params=_cparams(2),
        name="proj_qkv",
    )(x, nw, wqkv, cos_t, sa_t, sb_t)


def _ssd_scan_kernel(xs_ref, bm_ref, cm_ref, v_ref, etot_ref, wst_ref, qc_ref, dsk_ref, y_ref,
                     st_ref, yacc_ref, *, S):
    L = SSD_CHUNK
    nc = S // L
    J = SSD_HEADS_PER_GROUP
    n_hd = qc_ref.shape[2] // 2
    g0 = pl.program_id(1) * (2 * J)
    st_ref[...] = jnp.zeros(st_ref.shape, F32)

    lo = lax.broadcasted_iota(jnp.int32, (L, LANES), 1) < SSD_HEAD_DIM
    lo_row = lo[0:1]
    head_col = jnp.where(lo, 0, 1)
    zero_col = jnp.zeros((L, LANES), jnp.int32)
    sel = _head_selectors(L)

    ti = lax.broadcasted_iota(jnp.int32, (L, L), 0)
    si = lax.broadcasted_iota(jnp.int32, (L, L), 1)
    causal = (ti >= si, si >= ti)
    dsk = dsk_ref[0]

    def chunk(c, d):
        t0 = pl.multiple_of(c * L, L)
        xc = xs_ref[0, 0, pl.ds(t0, L), :]
        bc = bm_ref[0, 0, pl.ds(t0, L), :]
        cc = cm_ref[0, 0, pl.ds(t0, L), :]
        cb = _dot_nt(cc, bc)
        qc = qc_ref[0, pl.ds(t0, L), :]
        vv = v_ref[0, :, pl.ds(t0, L)]
        ww = wst_ref[0, :, pl.ds(t0, L)]
        bt = bc.T
        ms, bw = [], []
        for j in range(J):
            r = J * d + j
            seg = jnp.take_along_axis(qc, zero_col + (g0 + r), axis=1) - vv[r:r + 1, :]
            dec = jnp.exp2(jnp.where(causal[d], seg, -jnp.inf))
            ms.append((cb * dec).astype(BF16))
            bw.append(bt * ww[r:r + 1, :].astype(BF16))
        yds, dss = [], []
        for p in range(J // 2):
            xp = xc[:, p * LANES:(p + 1) * LANES]
            rhs = jnp.concatenate([xp * sel[0], xp * sel[1]], axis=0)
            yds.append(_dot(jnp.concatenate([ms[2 * p], ms[2 * p + 1]], axis=1), rhs))
            dss.append(_dot(jnp.concatenate([bw[2 * p], bw[2 * p + 1]], axis=1), rhs))
        yd = jnp.concatenate(yds, axis=1)
        ds = jnp.concatenate(dss, axis=1)

        escale = jnp.concatenate(
            [jnp.take_along_axis(qc, head_col + (n_hd + g0 + J * d + 2 * p), axis=1)
             for p in range(J // 2)], axis=1)
        st = st_ref[d]
        yo = _dot(cc, st.astype(BF16)) * escale
        et = etot_ref[0, J * d:J * d + J, pl.ds(t0, L)]
        drow = jnp.concatenate([jnp.where(lo_row, et[2 * p:2 * p + 1], et[2 * p + 1:2 * p + 2])
                                for p in range(J // 2)], axis=1)
        st_ref[d] = st * drow + ds
        return t0, yd + yo, xc.astype(F32)

    U = SSD_UNROLL

    def steps(i):
        return [(c, d) for u in range(U) for c, d in ((i * U + u, 0), (nc - 1 - i * U - u, 1))]

    def first_touch(i, carry):
        for c, d in steps(i):
            t0, y, _ = chunk(c, d)
            yacc_ref[pl.ds(t0, L), :] = y
        return carry

    def second_touch(i, carry):
        for c, d in steps(i):
            t0, y, xf = chunk(c, d)
            y_ref[0, 0, pl.ds(t0, L), :] = (yacc_ref[pl.ds(t0, L), :] + y + xf * dsk).astype(BF16)
        return carry

    lax.fori_loop(0, nc // 2 // U, first_touch, 0)
    lax.fori_loop(nc // 2 // U, nc // U, second_touch, 0)


def _ssd_scan(xs, bm, cm, v_t, etot_t, wst_t, qc, dsk):
    B, G, S, W = xs.shape
    N = SSD_D_STATE
    J2 = 2 * SSD_HEADS_PER_GROUP
    assert (S // SSD_CHUNK) % (2 * SSD_UNROLL) == 0
    grp = lambda w: pl.BlockSpec((1, 1, S, w), lambda b, g: (b, g, 0, 0))
    rows = pl.BlockSpec((1, J2, S), lambda b, g: (b, g, 0))
    return pl.pallas_call(
        functools.partial(_ssd_scan_kernel, S=S),
        grid=(B, G),
        in_specs=[grp(W), grp(N), grp(N), rows, rows, rows,
                  pl.BlockSpec((1, S, qc.shape[2]), lambda b, g: (b, 0, 0)),
                  pl.BlockSpec((1, 1, W), lambda b, g: (g, 0, 0))],
        out_specs=pl.BlockSpec((1, 1, S, W), lambda b, g: (b, g, 0, 0)),
        out_shape=jax.ShapeDtypeStruct((B, G, S, W), BF16),
        scratch_shapes=[pltpu.VMEM((2, N, W), F32), pltpu.VMEM((S, W), F32)],
        compiler_params=_cparams(2),
        name="ssd_scan",
    )(xs, bm, cm, v_t, etot_t, wst_t, qc, dsk)


def _dil_attn_kernel(*refs, S):
    n_g = len(ATT_DILS)
    qkv_refs, bias_ref, o_ref, scr = refs[:3 * n_g], refs[3 * n_g], refs[3 * n_g + 1], refs[3 * n_g + 2:]
    BQ, KW, HD = ATT_BQ, ATT_KW, ATT_HEAD_DIM
    lo_q = lax.broadcasted_iota(jnp.int32, (BQ, LANES), 1) < HD
    sel_q = _head_selectors(BQ)
    eye_q = (lax.broadcasted_iota(jnp.int32, (BQ, BQ), 0)
             == lax.broadcasted_iota(jnp.int32, (BQ, BQ), 1)).astype(F32).astype(BF16)
    sel_k = _head_selectors(KW)

    def attend(qb, kwin, vwin, bias_t):
        k_aug = jnp.concatenate([kwin, bias_t], axis=1)
        q_aug = jnp.concatenate(
            [jnp.concatenate([qb * sel_q[h], eye_q], axis=1) for h in range(2)], axis=0)
        s = _dot_nt(q_aug, k_aug)
        m = jnp.max(s, axis=-1, keepdims=True)
        p = jnp.exp2(s - m).astype(BF16)
        v_aug = jnp.concatenate([jnp.concatenate([vwin * sel_k[h], sel_k[h]], axis=1)
                                 for h in range(2)], axis=0)
        accl = _dot(jnp.concatenate([p[:BQ], p[BQ:]], axis=1), v_aug)
        return accl[:, :LANES], jnp.where(lo_q, m[:BQ], m[BQ:]), accl[:, LANES:]

    def block(g, n, it):
        q_ref, k_ref, v_ref = (qkv_refs[t * n_g + g] for t in range(3))
        nblk = n // BQ
        r = it // nblk
        qs = pl.multiple_of((it % nblk) * BQ, BQ)
        ks = pl.multiple_of(jnp.clip(qs - ATT_HALF, 0, n - KW), ATT_HALF)
        variant = (qs - ks) // ATT_HALF
        parts = attend(q_ref[0, r, 0, pl.ds(qs, BQ), :], k_ref[0, r, 0, pl.ds(ks, KW), :],
                       v_ref[0, r, 0, pl.ds(ks, KW), :], bias_ref[variant])
        return r, qs, parts

    for g in range(1, n_g):
        d = ATT_DILS[g]
        g_scr = scr[3 * (g - 1):3 * g]

        def dil_body(i, carry, g=g, d=d, g_scr=g_scr):
            for u in range(ATT_UNROLL):
                r, qs, parts = block(g, S // d, i * ATT_UNROLL + u)
                for ref, val in zip(g_scr, parts):
                    ref[pl.ds(r + d * qs, BQ, stride=d), :] = val
            return carry

        lax.fori_loop(0, S // BQ // ATT_UNROLL, dil_body, 0)

    def merge_body(i, carry):
        for u in range(ATT_UNROLL):
            _, qs, parts = block(0, S, i * ATT_UNROLL + u)
            groups = [parts] + [tuple(ref[pl.ds(qs, BQ), :] for ref in scr[3 * (g - 1):3 * g])
                                for g in range(1, len(ATT_DILS))]
            m = functools.reduce(jnp.maximum, [gm for _, gm, _ in groups])
            ws = [jnp.exp2(gm - m) for _, gm, _ in groups]
            num = functools.reduce(lambda a, b: a + b, [w * ga for w, (ga, _, _) in zip(ws, groups)])
            den = functools.reduce(lambda a, b: a + b, [w * gd for w, (_, _, gd) in zip(ws, groups)])
            o_ref[0, 0, pl.ds(qs, BQ), :] = (num / den).astype(BF16)
        return carry

    lax.fori_loop(0, S // BQ // ATT_UNROLL, merge_body, 0)


def _dil_attn(qkv, bias):
    B = qkv[0].shape[0]
    S = qkv[0].shape[3]
    W = ATT_GROUP_WIDTH
    in_specs = []
    for t in range(3):
        for d in ATT_DILS:
            in_specs.append(pl.BlockSpec((1, d, 1, S // d, LANES), lambda b, p: (b, 0, p, 0, 0)))
    in_specs.append(_full(bias.shape))
    n_scr = 3 * (len(ATT_DILS) - 1)
    return pl.pallas_call(
        functools.partial(_dil_attn_kernel, S=S),
        grid=(B, W // LANES),
        in_specs=in_specs,
        out_specs=pl.BlockSpec((1, 1, S, LANES), lambda b, p: (b, p, 0, 0)),
        out_shape=jax.ShapeDtypeStruct((B, W // LANES, S, LANES), BF16),
        scratch_shapes=[pltpu.VMEM((S, LANES), F32)] * n_scr,
        compiler_params=_cparams(2),
        name="dil_attn",
    )(*qkv, bias)


def _mix_merge_kernel(x_ref, ys_ref, zs_ref, ya_ref, gs_ref, nws_ref, ws_ref, wa_ref, wo_ref,
                      nwo_ref, o_ref):
    ys = jnp.concatenate([ys_ref[0, g] for g in range(ys_ref.shape[1])], axis=1)
    ya = jnp.concatenate([ya_ref[0, p] for p in range(ya_ref.shape[1])], axis=1)
    b = _dot(ya, wa_ref[...])
    u = ys.astype(F32) * zs_ref[...].astype(F32)
    un = _rms_scale(u, nws_ref[...]).astype(BF16)
    a = _dot(un, ws_ref[...])
    gs = gs_ref[...].astype(F32)
    merged = gs[:, :D_MODEL] * a + gs[:, D_MODEL:] * b
    mix = _dot(merged.astype(BF16), wo_ref[...])
    o_ref[...] = x_ref[...] + _rms_scale(mix, nwo_ref[...])


def _mix_merge(x2, ys, zs, ya, gs, nws, ws, wa, wo, nwo, *, tm=512):
    T, D = x2.shape
    nt = ys.shape[2] // tm
    tok = lambda w: pl.BlockSpec((tm, w), lambda i: (i, 0))
    slab = lambda a: pl.BlockSpec((1, a.shape[1], tm, a.shape[3]), lambda i: (i // nt, 0, i % nt, 0))
    return pl.pallas_call(
        _mix_merge_kernel,
        grid=(T // tm,),
        in_specs=[tok(D), slab(ys), tok(zs.shape[1]), slab(ya), tok(gs.shape[1]),
                  _full(nws.shape), _full(ws.shape), _full(wa.shape), _full(wo.shape),
                  _full(nwo.shape)],
        out_specs=tok(D),
        out_shape=jax.ShapeDtypeStruct((T, D), F32),
        compiler_params=_cparams(1),
        name="mix_merge",
    )(x2, ys, zs, ya, gs, nws, ws, wa, wo, nwo)


def _ffn_kernel(x_ref, nwi_ref, wg_ref, wu_ref, wd_ref, nwo_ref, o_ref, hn_ref, act_ref, *, th):
    x = x_ref[...]
    hn_ref[...] = _rms_scale(x, nwi_ref[...]).astype(BF16)
    for c in range(wg_ref.shape[1] // th):
        sl = slice(c * th, (c + 1) * th)
        gt = _dot(hn_ref[...], wg_ref[:, sl])
        up = _dot(hn_ref[...], wu_ref[:, sl])
        act_ref[:, sl] = (gt * _sigmoid(gt) * up).astype(BF16)
    y = _dot(act_ref[...], wd_ref[...])
    o_ref[...] = x + _rms_scale(y, nwo_ref[...])


def _ffn(x2, nwi, wg, wu, wd, nwo, *, tm=512, th=256):
    T, D = x2.shape
    Hf = wg.shape[1]
    tok = pl.BlockSpec((tm, D), lambda i: (i, 0))
    return pl.pallas_call(
        functools.partial(_ffn_kernel, th=th),
        grid=(T // tm,),
        in_specs=[tok, _full(nwi.shape), _full(wg.shape), _full(wu.shape), _full(wd.shape),
                  _full(nwo.shape)],
        out_specs=tok,
        out_shape=jax.ShapeDtypeStruct((T, D), F32),
        scratch_shapes=[pltpu.VMEM((tm, D), BF16), pltpu.VMEM((tm, Hf), BF16)],
        compiler_params=_cparams(1),
        name="ffn",
    )(x2, nwi, wg, wu, wd, nwo)


def _rope_tables(S):
    pos = jnp.arange(S, dtype=F32)
    inv_freq = ROPE_THETA ** (-jnp.arange(0, ROPE_DIM, 2, dtype=F32) / ROPE_DIM)
    ang = pos[:, None] * inv_freq[None, :]
    h = ROPE_DIM // 2
    cos, sin = jnp.cos(ang), jnp.sin(ang)
    pad = jnp.zeros((S, ATT_HEAD_DIM - ROPE_DIM), F32)
    zh = jnp.zeros((S, h), F32)
    cos_h = jnp.concatenate([cos, cos, pad + 1.0], axis=1)
    sa_h = jnp.concatenate([zh, sin, pad], axis=1)
    sb_h = jnp.concatenate([-sin, zh, pad], axis=1)
    rep = LANES // ATT_HEAD_DIM
    return tuple(jnp.tile(t, (1, rep)) for t in (cos_h, sa_h, sb_h))


def _attn_bias():
    i = np.arange(ATT_BQ)[None, :]
    j = np.arange(ATT_KW)[:, None]
    v = np.arange(3)[:, None, None]
    ok = np.abs(j - i - ATT_HALF * v) <= ATT_HALF
    return jnp.asarray(np.where(ok, 0.0, NEG_BIG), dtype=BF16)


def _layer(x, nmp, w_in, conv_w, conv_b, dt_bias, a_log, d_skip, ssd_nw, w_ssd, w_att, w_out,
           nmpost, nfpre, w_fin, w_fdown, nfpost):
    B, S, D = x.shape
    G, J = SSD_N_GROUPS, SSD_HEADS_PER_GROUP
    row = lambda v: v.reshape(1, -1).astype(F32)
    c0 = SSD_D_INNER
    c1 = c0 + SSD_D_INNER + 2 * SSD_BC_WIDTH
    c2 = c1 + 2 * SSD_N_HEADS
    c3 = c2 + 3 * ATT_WIDTH
    w16 = w_in.astype(BF16)
    perm = lambda v: v.reshape(v.shape[:-1] + (2, G, J)).swapaxes(-3, -2).reshape(v.shape[:-1] + (2 * G * J,))
    wdt = perm(w16[:, c1:c2])
    wdt = jnp.concatenate([wdt, jnp.zeros((D, LANES - wdt.shape[1]), wdt.dtype)], axis=1)
    wqkv = w16[:, c2:c3]
    wgt = w16[:, c3:]
    bias_col = perm(dt_bias.reshape(-1)).reshape(-1, 1).astype(F32)
    alog_col = perm(a_log.reshape(-1)).reshape(-1, 1).astype(F32)
    nw = row(nmp)

    zs, gs, v_t, etot_t, wst_t, qc = _proj_gate(x, nw, w16, wgt, wdt, bias_col, alog_col)
    cw8 = jnp.concatenate([conv_w, conv_b[None, :],
                           jnp.zeros((8 - SSD_CONV_WIDTH - 1, conv_w.shape[1]), F32)], axis=0)
    xs, bm, cm = _proj_xbc(x, nw, w16, cw8)
    qkv = _proj_qkv(x, nw, wqkv, *_rope_tables(S))

    dsk = jnp.repeat(d_skip.astype(F32), SSD_HEAD_DIM).reshape(G, 1, SSD_GROUP_WIDTH)
    y_ssd = _ssd_scan(xs, bm, cm, v_t, etot_t, wst_t, qc, dsk)
    y_att = _dil_attn(qkv, _attn_bias())

    T = B * S
    x1 = _mix_merge(x.reshape(T, D), y_ssd, zs.reshape(T, -1), y_att, gs.reshape(T, -1), row(ssd_nw), w_ssd.astype(BF16), w_att.astype(BF16),
                    w_out.astype(BF16), row(nmpost))
    x2 = _ffn(x1, row(nfpre), w_fin[:, :FFN_HIDDEN].astype(BF16), w_fin[:, FFN_HIDDEN:].astype(BF16),
              w_fdown.astype(BF16), row(nfpost))
    return x2.reshape(B, S, D)


def kernel(x, norm_mix_pre, w_in, ssd_conv_w, ssd_conv_b, ssd_dt_bias, ssd_A_log, ssd_D, ssd_norm_w,
           w_ssd_branch, w_attn_branch, w_out, norm_mix_post, norm_ffn_pre, w_ffn_in, w_ffn_down,
           norm_ffn_post):
    for l in range(w_in.shape[0]):
        x = _layer(x, norm_mix_pre[l], w_in[l], ssd_conv_w[l], ssd_conv_b[l], ssd_dt_bias[l],
                   ssd_A_log[l], ssd_D[l], ssd_norm_w[l], w_ssd_branch[l], w_attn_branch[l], w_out[l],
                   norm_mix_post[l], norm_ffn_pre[l], w_ffn_in[l], w_ffn_down[l], norm_ffn_post[l])
    return x
```
